```python
import numpy as np
import jax
import jax.numpy as jnp
from jax import lax

D_MODEL = 1024
BATCH = 1
SEQ = 16384
DEPTH = 1

N_META = 16
GRID_W = 64
EPS = 1e-6

MLA_HEADS = 8
MLA_NOPE = 128
MLA_ROPE = 64
MLA_V = 128
MLA_Q_RANK = 256
MLA_KV_RANK = 256
MLA_WIDTH = MLA_HEADS * MLA_V
ROPE_BASE = 10000.0
Q_BLOCK = 128

NA_HEADS = 16
NA_HEAD_DIM = 64
NA_WIDTH = NA_HEADS * NA_HEAD_DIM
NA_WIN_R = 8
NA_WIN_C = 16

D_INNER = MLA_WIDTH + NA_WIDTH
IN_SIZES = (MLA_Q_RANK, MLA_KV_RANK, MLA_ROPE, MLA_WIDTH, NA_WIDTH, NA_WIDTH, NA_WIDTH, NA_WIDTH)
D_IN_PROJ = sum(IN_SIZES)

kernel_name = "hybrid_mla_neighbourhood_attention_encoder_layer"


def _rmsnorm(x, w):
    xf = x.astype(jnp.float32)
    y = xf * lax.rsqrt(jnp.mean(xf * xf, axis=-1, keepdims=True) + EPS)
    return (y * w.astype(jnp.float32)).astype(x.dtype)


def _rope(x, cos, sin):
    x1, x2 = jnp.split(x.astype(jnp.float32), 2, axis=-1)
    return jnp.concatenate([x1 * cos - x2 * sin, x2 * cos + x1 * sin], axis=-1).astype(x.dtype)


def _mla(q_lat, kv_lat, k_pe, w_uq, w_ukv, q_lat_w, kv_lat_w, qn_w, qpe_w, kn_w, kpe_w):
    B, L, _ = q_lat.shape
    q = (_rmsnorm(q_lat, q_lat_w) @ w_uq).reshape(B, L, MLA_HEADS, MLA_NOPE + MLA_ROPE)
    kv = (_rmsnorm(kv_lat, kv_lat_w) @ w_ukv).reshape(B, L, MLA_HEADS, MLA_NOPE + MLA_V)
    q_nope, q_pe = q[..., :MLA_NOPE], q[..., MLA_NOPE:]
    k_nope, v = kv[..., :MLA_NOPE], kv[..., MLA_NOPE:]
    q_nope = _rmsnorm(q_nope, qn_w)
    k_nope = _rmsnorm(k_nope, kn_w)
    pos = jnp.arange(L, dtype=jnp.float32)
    inv_freq = ROPE_BASE ** (-(jnp.arange(0, MLA_ROPE, 2, dtype=jnp.float32) / MLA_ROPE))
    ang = pos[:, None] * inv_freq[None, :]
    cos, sin = jnp.cos(ang), jnp.sin(ang)
    q_pe = _rope(_rmsnorm(q_pe, qpe_w), cos[None, :, None, :], sin[None, :, None, :])
    k_pe = _rope(_rmsnorm(k_pe, kpe_w), cos[None], sin[None])
    scale = (MLA_NOPE + MLA_ROPE) ** -0.5

    n_blk = -(-L // Q_BLOCK)
    pad = n_blk * Q_BLOCK - L
    qn = jnp.pad(q_nope, ((0, 0), (0, pad), (0, 0), (0, 0)))
    qn = qn.reshape(B, n_blk, Q_BLOCK, MLA_HEADS, MLA_NOPE).transpose(1, 0, 2, 3, 4)
    qp = jnp.pad(q_pe, ((0, 0), (0, pad), (0, 0), (0, 0)))
    qp = qp.reshape(B, n_blk, Q_BLOCK, MLA_HEADS, MLA_ROPE).transpose(1, 0, 2, 3, 4)

    def block(args):
        qn_b, qp_b = args
        s = (jnp.einsum('bqhd,bkhd->bhqk', qn_b, k_nope)
             + jnp.einsum('bqhr,bkr->bhqk', qp_b, k_pe))
        p = jax.nn.softmax(s.astype(jnp.float32) * scale, axis=-1).astype(v.dtype)
        return jnp.einsum('bhqk,bkhd->bqhd', p, v)

    o = lax.map(block, (qn, qp))
    o = o.transpose(1, 0, 2, 3, 4).reshape(B, n_blk * Q_BLOCK, MLA_WIDTH)
    return o[:, :L]


def _na(q, k, v, q_w, k_w, rel_bias, meta_bias):
    B, L, _ = q.shape
    n_real = L - N_META
    rows = n_real // GRID_W
    win_r = min(NA_WIN_R, rows)
    n_win = win_r * NA_WIN_C
    scale = NA_HEAD_DIM ** -0.5
    q = _rmsnorm(q.reshape(B, L, NA_HEADS, NA_HEAD_DIM), q_w)
    k = _rmsnorm(k.reshape(B, L, NA_HEADS, NA_HEAD_DIM), k_w)
    v = v.reshape(B, L, NA_HEADS, NA_HEAD_DIM)
    q_meta, k_meta, v_meta = q[:, :N_META], k[:, :N_META], v[:, :N_META]
    q_grid = q[:, N_META:].reshape(B, rows, GRID_W, NA_HEADS, NA_HEAD_DIM)
    k_grid = k[:, N_META:].reshape(B, rows, GRID_W, NA_HEADS, NA_HEAD_DIM)
    v_grid = v[:, N_META:].reshape(B, rows, GRID_W, NA_HEADS, NA_HEAD_DIM)

    cols = np.arange(GRID_W)
    c0 = np.clip(cols - NA_WIN_C // 2, 0, GRID_W - NA_WIN_C)
    col_idx_np = c0[:, None] + np.arange(NA_WIN_C)[None, :]
    col_idx = jnp.asarray(col_idx_np, dtype=jnp.int32)
    dc_idx = jnp.asarray(col_idx_np - cols[:, None] + (NA_WIN_C - 1), dtype=jnp.int32)
    meta_b = meta_bias.astype(jnp.float32)[None, :, None, :]

    def row_block(args):
        r, q_row = args
        r0 = jnp.clip(r - win_r // 2, 0, rows - win_r)
        k_rows = lax.dynamic_slice_in_dim(k_grid, r0, win_r, axis=1)
        v_rows = lax.dynamic_slice_in_dim(v_grid, r0, win_r, axis=1)
        k_win = k_rows[:, :, col_idx]
        v_win = v_rows[:, :, col_idx]
        dr_idx = r0 + jnp.arange(win_r, dtype=jnp.int32) - r + (NA_WIN_R - 1)
        bias = rel_bias[:, dr_idx[None, :, None], dc_idx[:, None, :]]
        s_win = (jnp.einsum('bqhd,brqchd->bhqrc', q_row, k_win).astype(jnp.float32) * scale
                 + bias.astype(jnp.float32)[None])
        s_meta = jnp.einsum('bqhd,bmhd->bhqm', q_row, k_meta).astype(jnp.float32) * scale + meta_b
        s = jnp.concatenate([s_win.reshape(B, NA_HEADS, GRID_W, n_win), s_meta], axis=-1)
        p = jax.nn.softmax(s, axis=-1).astype(v.dtype)
        p_win = p[..., :n_win].reshape(B, NA_HEADS, GRID_W, win_r, NA_WIN_C)
        p_meta = p[..., n_win:]
        return (jnp.einsum('bhqrc,brqchd->bqhd', p_win, v_win)
                + jnp.einsum('bhqm,bmhd->bqhd', p_meta, v_meta))

    o_grid = lax.map(row_block, (jnp.arange(rows, dtype=jnp.int32), q_grid.transpose(1, 0, 2, 3, 4)))
    o_real = o_grid.transpose(1, 0, 2, 3, 4).reshape(B, n_real, NA_WIDTH)

    s_mm = jnp.einsum('bqhd,bmhd->bhqm', q_meta, k_meta).astype(jnp.float32) * scale + meta_b
    p_mm = jax.nn.softmax(s_mm, axis=-1).astype(v.dtype)
    o_meta = jnp.einsum('bhqm,bmhd->bqhd', p_mm, v_meta).reshape(B, N_META, NA_WIDTH)
    return jnp.concatenate([o_meta, o_real], axis=1)


def setup_inputs(seed: int = 0) -> dict:
    key = jax.random.key(seed)
    ks = jax.random.split(key, 20)
    f32 = jnp.float32

    def nrm(k, shape, scale):
        return jax.random.normal(k, shape, f32) * scale

    def gain(k, shape):
        return 1.0 + 0.05 * jax.random.normal(k, shape, f32)

    return {
        "x": jax.random.normal(ks[0], (BATCH, SEQ, D_MODEL), f32),
        "meta_tokens": nrm(ks[1], (N_META, D_MODEL), 1.0),
        "norm_w": gain(ks[2], (DEPTH, D_MODEL)),
        "w_in": nrm(ks[3], (DEPTH, D_MODEL, D_IN_PROJ), D_MODEL ** -0.5),
        "q_lat_norm_w": gain(ks[4], (DEPTH, MLA_Q_RANK)),
        "kv_lat_norm_w": gain(ks[5], (DEPTH, MLA_KV_RANK)),
        "w_uq": nrm(ks[6], (DEPTH, MLA_Q_RANK, MLA_HEADS * (MLA_NOPE + MLA_ROPE)), MLA_Q_RANK ** -0.5),
        "w_ukv": nrm(ks[7], (DEPTH, MLA_KV_RANK, MLA_HEADS * (MLA_NOPE + MLA_V)), MLA_KV_RANK ** -0.5),
        "mla_qn_w": gain(ks[8], (DEPTH, MLA_NOPE)),
        "mla_qpe_w": gain(ks[9], (DEPTH, MLA_ROPE)),
        "mla_kn_w": gain(ks[10], (DEPTH, MLA_NOPE)),
        "mla_kpe_w": gain(ks[11], (DEPTH, MLA_ROPE)),
        "na_q_norm_w": gain(ks[12], (DEPTH, NA_HEAD_DIM)),
        "na_k_norm_w": gain(ks[13], (DEPTH, NA_HEAD_DIM)),
        "na_rel_bias": nrm(ks[14], (DEPTH, NA_HEADS, 2 * NA_WIN_R - 1, 2 * NA_WIN_C - 1), 0.5),
        "na_meta_bias": nrm(ks[15], (DEPTH, NA_HEADS, N_META), 0.5),
        "w_out": nrm(ks[16], (DEPTH, D_INNER, D_MODEL), D_INNER ** -0.5),
    }


def reference(x, meta_tokens, norm_w, w_in, q_lat_norm_w, kv_lat_norm_w, w_uq, w_ukv,
              mla_qn_w, mla_qpe_w, mla_kn_w, mla_kpe_w, na_q_norm_w, na_k_norm_w,
              na_rel_bias, na_meta_bias, w_out):
    B = x.shape[0]
    meta = jnp.broadcast_to(meta_tokens.astype(x.dtype)[None], (B, N_META, D_MODEL))
    h_res = jnp.concatenate([meta, x], axis=1)
    split_at = [int(s) for s in np.cumsum(IN_SIZES)[:-1]]
    for l in range(DEPTH):
        h = _rmsnorm(h_res, norm_w[l])
        proj = h @ w_in[l]
        q_lat, kv_lat, k_pe, g_mla, na_q, na_k, na_v, g_na = jnp.split(proj, split_at, axis=-1)
        y_mla = _mla(q_lat, kv_lat, k_pe, w_uq[l], w_ukv[l], q_lat_norm_w[l], kv_lat_norm_w[l],
                     mla_qn_w[l], mla_qpe_w[l], mla_kn_w[l], mla_kpe_w[l]) * jax.nn.silu(g_mla)
        y_na = _na(na_q, na_k, na_v, na_q_norm_w[l], na_k_norm_w[l],
                   na_rel_bias[l], na_meta_bias[l]) * jax.nn.silu(g_na)
        h_res = h_res + jnp.concatenate([y_mla, y_na], axis=-1) @ w_out[l]
    return h_res[:, N_META:]
```

```python
import functools
import math

import numpy as np
import jax
import jax.numpy as jnp
from jax import lax
from jax.experimental import pallas as pl
from jax.experimental.pallas import tpu as pltpu

D_MODEL = 1024
SEQ = 16384
N_META = 16
GRID_W = 64
ROWS = SEQ // GRID_W
EPS = 1e-6

MLA_HEADS = 8
MLA_NOPE = 128
MLA_ROPE = 64
MLA_V = 128
MLA_Q_RANK = 256
MLA_KV_RANK = 256
MLA_WIDTH = MLA_HEADS * MLA_V
ROPE_BASE = 10000.0

NA_HEADS = 16
NA_HEAD_DIM = 64
NA_WIDTH = NA_HEADS * NA_HEAD_DIM
NA_WIN_R = 8
NA_WIN_C = 16

LOG2E = math.log2(math.e)
NEG = -1e30

LANES = 128
MXU_DIM = 256
VMEM_LIMIT_BYTES = 56 * 1024 * 1024

META_PAD = 128
QK_DIM = MXU_DIM
V_ROWS = MLA_V + 16

NA_GROUP = 4
NA_SLAB_ROWS = 10
NA_SLAB = NA_SLAB_ROWS * GRID_W
NA_ROWS_PER_STEP = 8

_NT = (((1,), (1,)), ((), ()))
_TN = (((0,), (0,)), ((), ()))

f32 = jnp.float32
bf16 = jnp.bfloat16


def _rms(x, axis):
    return lax.rsqrt(jnp.mean(x * x, axis=axis, keepdims=True) + EPS)


def _silu(g):
    return g / (1.0 + jnp.exp(-g))


def _proj_kernel(x_ref, cos_ref, sin_ref, nw_ref, wa_ref, wbt_ref, qlw_ref, kvlw_ref,
                 wuqt_ref, wuk_ref, wuvt_ref, qnw_ref, qpew_ref, knw_ref, kpew_ref,
                 naqw_ref, nakw_ref,
                 qt_ref, k_ref, vt_ref, gmt_ref, naq_ref, nak_ref, navt_ref, gnt_ref):
    tm = x_ref.shape[0]
    x = x_ref[...]
    h = (x * _rms(x, 1) * nw_ref[...]).astype(bf16)
    pa = jnp.dot(h, wa_ref[...], preferred_element_type=f32)
    pbt = lax.dot_general(wbt_ref[...], h, _NT, preferred_element_type=f32)

    gmt_ref[...] = pbt[0:MLA_WIDTH].astype(bf16)
    navt_ref[...] = pbt[MLA_WIDTH:MLA_WIDTH + NA_WIDTH].astype(bf16)
    gnt_ref[...] = pbt[MLA_WIDTH + NA_WIDTH:MLA_WIDTH + 2 * NA_WIDTH].astype(bf16)

    cos = cos_ref[...]
    sin = sin_ref[...]
    half = MLA_ROPE // 2

    kpe = pbt[MLA_WIDTH + 2 * NA_WIDTH:MLA_WIDTH + 2 * NA_WIDTH + MLA_ROPE]
    kpe = kpe * _rms(kpe, 0) * kpew_ref[...]
    k1, k2 = kpe[:half], kpe[half:]
    kro = jnp.concatenate(
        [k1 * cos - k2 * sin, k2 * cos + k1 * sin, jnp.zeros((LANES - MLA_ROPE, tm), f32)], axis=0)
    kpe_nat = kro.T.astype(bf16)

    qscale = (MLA_NOPE + MLA_ROPE) ** -0.5 * LOG2E
    ql = pa[:, 0:MLA_Q_RANK]
    qln = (ql * _rms(ql, 1) * qlw_ref[...]).astype(bf16)
    qt = lax.dot_general(wuqt_ref[...], qln, _NT, preferred_element_type=f32)
    per_head = MLA_NOPE + MLA_ROPE
    for hh in range(MLA_HEADS):
        base = per_head * hh
        qn = qt[base:base + MLA_NOPE]
        qn = qn * _rms(qn, 0) * qnw_ref[...]
        qp = qt[base + MLA_NOPE:base + per_head]
        qp = qp * _rms(qp, 0) * qpew_ref[...]
        q1, q2 = qp[:half], qp[half:]
        qt_ref[hh, 0:MLA_NOPE, :] = (qn * qscale).astype(bf16)
        qt_ref[hh, MLA_NOPE:MLA_NOPE + half, :] = ((q1 * cos - q2 * sin) * qscale).astype(bf16)
        qt_ref[hh, MLA_NOPE + half:per_head, :] = ((q2 * cos + q1 * sin) * qscale).astype(bf16)
        qt_ref[hh, per_head:QK_DIM, :] = jnp.zeros((QK_DIM - per_head, tm), bf16)

    kvl = pa[:, MLA_Q_RANK:MLA_Q_RANK + MLA_KV_RANK]
    kvn = (kvl * _rms(kvl, 1) * kvlw_ref[...]).astype(bf16)
    kn = jnp.dot(kvn, wuk_ref[...], preferred_element_type=f32)
    vt = lax.dot_general(wuvt_ref[...], kvn, _NT, preferred_element_type=f32)
    for hh in range(MLA_HEADS):
        xh = kn[:, MLA_NOPE * hh:MLA_NOPE * (hh + 1)]
        xh = xh * _rms(xh, 1) * knw_ref[...]
        k_ref[hh, :, 0:MLA_NOPE] = xh.astype(bf16)
        k_ref[hh, :, MLA_NOPE:QK_DIM] = kpe_nat
        vt_ref[hh, 0:MLA_V, :] = vt[MLA_V * hh:MLA_V * (hh + 1)].astype(bf16)
        vt_ref[hh, MLA_V:V_ROWS, :] = jnp.ones((V_ROWS - MLA_V, tm), bf16)

    lane = lax.broadcasted_iota(jnp.int32, (tm, LANES), 1)
    lo = lane < NA_HEAD_DIM
    na_scale = NA_HEAD_DIM ** -0.5 * LOG2E
    q_off = MLA_Q_RANK + MLA_KV_RANK
    k_off = q_off + NA_WIDTH
    for p in range(NA_WIDTH // LANES):
        for off, w_ref, dst, scale in ((q_off, naqw_ref, naq_ref, na_scale), (k_off, nakw_ref, nak_ref, 1.0)):
            xp = pa[:, off + LANES * p:off + LANES * (p + 1)]
            sq = xp * xp
            s_lo = jnp.sum(jnp.where(lo, sq, 0.0), axis=1, keepdims=True)
            s_hi = jnp.sum(jnp.where(lo, 0.0, sq), axis=1, keepdims=True)
            r = jnp.where(lo, lax.rsqrt(s_lo / NA_HEAD_DIM + EPS), lax.rsqrt(s_hi / NA_HEAD_DIM + EPS))
            y = xp * r * w_ref[...]
            if scale != 1.0:
                y = y * scale
            dst[:, LANES * p:LANES * (p + 1)] = y.astype(bf16)


def _const_spec(shape):
    nd = len(shape)
    return pl.BlockSpec(shape, lambda i, _nd=nd: (0,) * _nd, pipeline_mode=pl.Buffered(1))


def _project(x, cos_t, sin_t, w, tm):
    n = x.shape[0]
    grid = (n // tm,)
    weights = (w["norm_w"], w["wa"], w["wbt"], w["qlw"], w["kvlw"], w["wuqt"], w["wuk"], w["wuvt"],
               w["qnw"], w["qpew"], w["knw"], w["kpew"], w["naqw"], w["nakw"])
    in_specs = [pl.BlockSpec((tm, D_MODEL), lambda i: (i, 0)),
                pl.BlockSpec((MLA_ROPE // 2, tm), lambda i: (0, i)),
                pl.BlockSpec((MLA_ROPE // 2, tm), lambda i: (0, i))]
    in_specs += [_const_spec(a.shape) for a in weights]
    out_shape = (
        jax.ShapeDtypeStruct((MLA_HEADS, QK_DIM, n), bf16),
        jax.ShapeDtypeStruct((MLA_HEADS, n, QK_DIM), bf16),
        jax.ShapeDtypeStruct((MLA_HEADS, V_ROWS, n), bf16),
        jax.ShapeDtypeStruct((MLA_WIDTH, n), bf16),
        jax.ShapeDtypeStruct((n, NA_WIDTH), bf16),
        jax.ShapeDtypeStruct((n, NA_WIDTH), bf16),
        jax.ShapeDtypeStruct((NA_WIDTH, n), bf16),
        jax.ShapeDtypeStruct((NA_WIDTH, n), bf16),
    )
    out_specs = (
        pl.BlockSpec((MLA_HEADS, QK_DIM, tm), lambda i: (0, 0, i)),
        pl.BlockSpec((MLA_HEADS, tm, QK_DIM), lambda i: (0, i, 0)),
        pl.BlockSpec((MLA_HEADS, V_ROWS, tm), lambda i: (0, 0, i)),
        pl.BlockSpec((MLA_WIDTH, tm), lambda i: (0, i)),
        pl.BlockSpec((tm, NA_WIDTH), lambda i: (i, 0)),
        pl.BlockSpec((tm, NA_WIDTH), lambda i: (i, 0)),
        pl.BlockSpec((NA_WIDTH, tm), lambda i: (0, i)),
        pl.BlockSpec((NA_WIDTH, tm), lambda i: (0, i)),
    )
    return pl.pallas_call(
        _proj_kernel,
        grid=grid,
        in_specs=in_specs,
        out_specs=out_specs,
        out_shape=out_shape,
        compiler_params=pltpu.CompilerParams(
            dimension_semantics=("arbitrary",), vmem_limit_bytes=VMEM_LIMIT_BYTES),
        name="proj",
    )(x, cos_t, sin_t, *weights)


def _mla_kernel(qt_ref, k_ref, vt_ref, km_ref, vtm_ref, g_ref, o_ref, acc_ref, m_ref, *, tk):
    qt = qt_ref[0]
    nk = k_ref.shape[1] // tk

    s = jnp.dot(km_ref[0], qt, preferred_element_type=f32)
    row = lax.broadcasted_iota(jnp.int32, s.shape, 0)
    s = jnp.where(row < N_META, s, NEG)
    m0 = jnp.max(s, axis=0, keepdims=True)
    p = jnp.exp2(s - m0).astype(bf16)
    acc_ref[...] = jnp.dot(vtm_ref[0], p, preferred_element_type=f32)
    m_ref[...] = m0

    def body(i, carry):
        off = pl.multiple_of(i * tk, tk)
        s = jnp.dot(k_ref[0, pl.ds(off, tk), :], qt, preferred_element_type=f32)
        m_prev = m_ref[...]
        m_new = jnp.maximum(m_prev, jnp.max(s, axis=0, keepdims=True))
        alpha = jnp.exp2(m_prev - m_new)
        p = jnp.exp2(s - m_new).astype(bf16)
        pv = jnp.dot(vt_ref[0, :, pl.ds(off, tk)], p, preferred_element_type=f32)
        acc_ref[...] = alpha * acc_ref[...] + pv
        m_ref[...] = m_new
        return carry

    lax.fori_loop(0, nk, body, 0)

    acc = acc_ref[...]
    o = acc[0:MLA_V] / acc[MLA_V:MLA_V + 1]
    g = g_ref[...].astype(f32)
    o_ref[...] = (o * _silu(g)).astype(bf16)


def _mla(qt, k, vt, km, vtm, gmt, tq, tk):
    n = qt.shape[2]
    grid = (MLA_HEADS, n // tq)
    return pl.pallas_call(
        functools.partial(_mla_kernel, tk=tk),
        grid=grid,
        in_specs=[
            pl.BlockSpec((1, QK_DIM, tq), lambda h, j: (h, 0, j)),
            pl.BlockSpec((1, n, QK_DIM), lambda h, j: (h, 0, 0)),
            pl.BlockSpec((1, V_ROWS, n), lambda h, j: (h, 0, 0)),
            pl.BlockSpec((1, META_PAD, QK_DIM), lambda h, j: (h, 0, 0)),
            pl.BlockSpec((1, V_ROWS, META_PAD), lambda h, j: (h, 0, 0)),
            pl.BlockSpec((MLA_V, tq), lambda h, j: (h, j)),
        ],
        out_specs=pl.BlockSpec((MLA_V, tq), lambda h, j: (h, j)),
        out_shape=jax.ShapeDtypeStruct((MLA_WIDTH, n), bf16),
        scratch_shapes=[pltpu.VMEM((V_ROWS, tq), f32), pltpu.VMEM((1, tq), f32)],
        compiler_params=pltpu.CompilerParams(
            dimension_semantics=("arbitrary", "arbitrary"), vmem_limit_bytes=VMEM_LIMIT_BYTES),
        name="mla",
    )(qt, k, vt, km, vtm, gmt)


def _na_row_plan():
    r = np.arange(ROWS)
    r0 = np.clip(r - NA_WIN_R // 2, 0, ROWS - NA_WIN_R)
    r0e = np.minimum(r0 - (r0 % 2), ROWS - NA_SLAB_ROWS)
    e = r0e - r + (NA_WIN_R - 1)
    d = r0 - r0e
    variants = sorted(set(zip(e.tolist(), d.tolist())))
    vidx = np.array([variants.index((a, b)) for a, b in zip(e.tolist(), d.tolist())], np.int32)
    return r0e.astype(np.int32), vidx, variants


def _na_tables(rel_bias, meta_bias):
    _, _, variants = _na_row_plan()
    j = np.arange(NA_SLAB_ROWS)
    dr = np.stack([e + j for e, _ in variants])
    valid_j = np.stack([(j >= d) & (j < d + NA_WIN_R) for _, d in variants])
    c = np.arange(GRID_W)
    c0 = np.clip(c - NA_WIN_C // 2, 0, GRID_W - NA_WIN_C)
    kc = np.arange(GRID_W)
    valid_c = (kc[:, None] >= c0[None, :]) & (kc[:, None] < c0[None, :] + NA_WIN_C)
    dc = kc[:, None] - c[None, :] + (NA_WIN_C - 1)
    dr_i = np.clip(dr, 0, 2 * NA_WIN_R - 2)
    dc_i = np.clip(dc, 0, 2 * NA_WIN_C - 2)
    valid = valid_j[:, :, None, None] & valid_c[None, None, :, :]
    g = rel_bias.astype(f32)[:, dr_i[:, :, None, None], dc_i[None, None, :, :]]
    t = jnp.where(valid[None], g * LOG2E, NEG)
    nv = len(variants)
    ng = NA_HEADS // NA_GROUP
    t = t.reshape(ng, NA_GROUP, nv, NA_SLAB_ROWS, GRID_W, GRID_W)
    t = t.transpose(0, 2, 3, 4, 1, 5).reshape(ng, nv, NA_SLAB, NA_GROUP * GRID_W)
    mb = meta_bias.astype(f32) * LOG2E
    mb = jnp.concatenate([mb, jnp.full((NA_HEADS, META_PAD - N_META), NEG, f32)], axis=1)
    tm = jnp.broadcast_to(mb.reshape(ng, NA_GROUP, META_PAD, 1), (ng, NA_GROUP, META_PAD, GRID_W))
    tm = tm.transpose(0, 2, 1, 3).reshape(ng, META_PAD, NA_GROUP * GRID_W)
    return t, tm


def _na_kernel(r0e_ref, vidx_ref, q_ref, k_ref, vt_ref, km_ref, vtm_ref, t_ref, tm_ref, g_ref, o_ref):
    b = pl.program_id(1)
    gw = NA_GROUP * GRID_W
    lane_head = lax.broadcasted_iota(jnp.int32, (GRID_W, gw), 1) // GRID_W
    row_head = lax.broadcasted_iota(jnp.int32, (gw, gw), 0) // NA_HEAD_DIM
    col_head = lax.broadcasted_iota(jnp.int32, (gw, gw), 1) // GRID_W
    diag = row_head == col_head
    lane_lo = lax.broadcasted_iota(jnp.int32, (gw, LANES), 1) < GRID_W

    def one_row(r, rl):
        q = q_ref[pl.ds(pl.multiple_of(rl * GRID_W, GRID_W), GRID_W), :].astype(f32)
        wt = jnp.concatenate(
            [jnp.where(lane_head == hp, q, 0.0) for hp in range(NA_GROUP)], axis=0).astype(bf16)
        r0e = r0e_ref[r]
        v = vidx_ref[r]
        koff = pl.multiple_of(r0e * GRID_W, LANES)
        ks = k_ref[pl.ds(koff, NA_SLAB), :]
        s = lax.dot_general(ks, wt, _NT, preferred_element_type=f32) + t_ref[0, v]
        sm = lax.dot_general(km_ref[...], wt, _NT, preferred_element_type=f32) + tm_ref[0]
        m = jnp.maximum(jnp.max(s, axis=0, keepdims=True), jnp.max(sm, axis=0, keepdims=True))
        p = jnp.exp2(s - m)
        pm = jnp.exp2(sm - m)
        l = jnp.sum(p, axis=0, keepdims=True) + jnp.sum(pm, axis=0, keepdims=True)
        out = jnp.dot(vt_ref[:, pl.ds(koff, NA_SLAB)], p.astype(bf16), preferred_element_type=f32)
        out = out + jnp.dot(vtm_ref[...], pm.astype(bf16), preferred_element_type=f32)
        out = out / l
        x = jnp.where(diag, out, 0.0)
        c = x[:, :LANES] + x[:, LANES:]
        return c + pltpu.roll(c, GRID_W, axis=1)

    def pair(pi, carry):
        r = b * NA_ROWS_PER_STEP + 2 * pi
        d0 = one_row(r, 2 * pi)
        d1 = one_row(r + 1, 2 * pi + 1)
        y = jnp.where(lane_lo, d0, d1)
        loff = pl.multiple_of(pi * LANES, LANES)
        g = g_ref[:, pl.ds(loff, LANES)].astype(f32)
        o_ref[:, pl.ds(loff, LANES)] = (y * _silu(g)).astype(bf16)
        return carry

    lax.fori_loop(0, NA_ROWS_PER_STEP // 2, pair, 0)


def _na(naq, nak, navt, nakm, navtm, t, tmeta, gnt):
    r0e, vidx, variants = _na_row_plan()
    nv = len(variants)
    ng = NA_HEADS // NA_GROUP
    gw = NA_GROUP * GRID_W
    tq = NA_ROWS_PER_STEP * GRID_W
    grid = (ng, ROWS // NA_ROWS_PER_STEP)
    grid_spec = pltpu.PrefetchScalarGridSpec(
        num_scalar_prefetch=2,
        grid=grid,
        in_specs=[
            pl.BlockSpec((tq, gw), lambda g, b, *_: (b, g)),
            pl.BlockSpec((SEQ, gw), lambda g, b, *_: (0, g)),
            pl.BlockSpec((gw, SEQ), lambda g, b, *_: (g, 0)),
            pl.BlockSpec((META_PAD, gw), lambda g, b, *_: (0, g)),
            pl.BlockSpec((gw, META_PAD), lambda g, b, *_: (g, 0)),
            pl.BlockSpec((1, nv, NA_SLAB, gw), lambda g, b, *_: (g, 0, 0, 0)),
            pl.BlockSpec((1, META_PAD, gw), lambda g, b, *_: (g, 0, 0)),
            pl.BlockSpec((gw, tq), lambda g, b, *_: (g, b)),
        ],
        out_specs=pl.BlockSpec((gw, tq), lambda g, b, *_: (g, b)),
    )
    return pl.pallas_call(
        _na_kernel,
        grid_spec=grid_spec,
        out_shape=jax.ShapeDtypeStruct((NA_WIDTH, SEQ), bf16),
        compiler_params=pltpu.CompilerParams(
            dimension_semantics=("arbitrary", "arbitrary"), vmem_limit_bytes=VMEM_LIMIT_BYTES),
        name="na",
    )(jnp.asarray(r0e), jnp.asarray(vidx), naq, nak, navt, nakm, navtm, t, tmeta, gnt)


def _out_kernel(x_ref, ym_ref, yn_ref, wm_ref, wn_ref, o_ref):
    acc = lax.dot_general(ym_ref[...], wm_ref[...], _TN, preferred_element_type=f32)
    acc = acc + lax.dot_general(yn_ref[...], wn_ref[...], _TN, preferred_element_type=f32)
    o_ref[...] = x_ref[...] + acc


def _out_proj(x, ymt, ynt, wm, wn, tm):
    n = x.shape[0]
    return pl.pallas_call(
        _out_kernel,
        grid=(n // tm,),
        in_specs=[
            pl.BlockSpec((tm, D_MODEL), lambda i: (i, 0)),
            pl.BlockSpec((MLA_WIDTH, tm), lambda i: (0, i)),
            pl.BlockSpec((NA_WIDTH, tm), lambda i: (0, i)),
            _const_spec(wm.shape),
            _const_spec(wn.shape),
        ],
        out_specs=pl.BlockSpec((tm, D_MODEL), lambda i: (i, 0)),
        out_shape=jax.ShapeDtypeStruct((n, D_MODEL), f32),
        compiler_params=pltpu.CompilerParams(
            dimension_semantics=("arbitrary",), vmem_limit_bytes=VMEM_LIMIT_BYTES),
        name="out_proj",
    )(x, ymt, ynt, wm, wn)


def _prep_weights(norm_w, w_in, q_lat_norm_w, kv_lat_norm_w, w_uq, w_ukv,
                  mla_qn_w, mla_qpe_w, mla_kn_w, mla_kpe_w, na_q_norm_w, na_k_norm_w):
    sizes = (MLA_Q_RANK, MLA_KV_RANK, MLA_ROPE, MLA_WIDTH, NA_WIDTH, NA_WIDTH, NA_WIDTH, NA_WIDTH)
    o = np.concatenate([[0], np.cumsum(sizes)])
    seg = [w_in[:, o[i]:o[i + 1]] for i in range(8)]
    wa = jnp.concatenate([seg[0], seg[1], seg[4], seg[5]], axis=1).astype(bf16)
    wbt = jnp.concatenate([seg[3], seg[6], seg[7], seg[2]], axis=1).T.astype(bf16)
    ukv = w_ukv.reshape(MLA_KV_RANK, MLA_HEADS, 2, MLA_NOPE)
    return {
        "norm_w": norm_w.reshape(1, D_MODEL).astype(f32),
        "wa": wa,
        "wbt": wbt,
        "qlw": q_lat_norm_w.reshape(1, MLA_Q_RANK).astype(f32),
        "kvlw": kv_lat_norm_w.reshape(1, MLA_KV_RANK).astype(f32),
        "wuqt": w_uq.T.astype(bf16),
        "wuk": ukv[:, :, 0, :].reshape(MLA_KV_RANK, MLA_HEADS * MLA_NOPE).astype(bf16),
        "wuvt": ukv[:, :, 1, :].reshape(MLA_KV_RANK, MLA_HEADS * MLA_V).T.astype(bf16),
        "qnw": mla_qn_w.reshape(MLA_NOPE, 1).astype(f32),
        "qpew": mla_qpe_w.reshape(MLA_ROPE, 1).astype(f32),
        "knw": mla_kn_w.reshape(1, MLA_NOPE).astype(f32),
        "kpew": mla_kpe_w.reshape(MLA_ROPE, 1).astype(f32),
        "naqw": jnp.tile(na_q_norm_w.reshape(1, NA_HEAD_DIM), (1, 2)).astype(f32),
        "nakw": jnp.tile(na_k_norm_w.reshape(1, NA_HEAD_DIM), (1, 2)).astype(f32),
    }


def kernel(x, meta_tokens, norm_w, w_in, q_lat_norm_w, kv_lat_norm_w, w_uq, w_ukv,
           mla_qn_w, mla_qpe_w, mla_kn_w, mla_kpe_w, na_q_norm_w, na_k_norm_w,
           na_rel_bias, na_meta_bias, w_out):
    assert x.shape == (1, SEQ, D_MODEL) and norm_w.shape[0] == 1
    xr = x[0]
    w = _prep_weights(norm_w[0], w_in[0], q_lat_norm_w[0], kv_lat_norm_w[0], w_uq[0], w_ukv[0],
                      mla_qn_w[0], mla_qpe_w[0], mla_kn_w[0], mla_kpe_w[0],
                      na_q_norm_w[0], na_k_norm_w[0])

    pos = jnp.arange(N_META + SEQ, dtype=f32)
    inv_freq = ROPE_BASE ** (-(jnp.arange(0, MLA_ROPE, 2, dtype=f32) / MLA_ROPE))
    ang = pos[:, None] * inv_freq[None, :]
    cos_t, sin_t = jnp.cos(ang).T, jnp.sin(ang).T
    pad = ((0, 0), (0, META_PAD - N_META))
    cos_m, sin_m = jnp.pad(cos_t[:, :N_META], pad), jnp.pad(sin_t[:, :N_META], pad)

    xm = jnp.pad(meta_tokens.astype(f32), ((0, META_PAD - N_META), (0, 0)))

    qt, k, vt, gmt, naq, nak, navt, gnt = _project(xr, cos_t[:, N_META:], sin_t[:, N_META:], w, tm=256)
    _, km, vtm, _, _, nakm, navtm, _ = _project(xm, cos_m, sin_m, w, tm=META_PAD)

    ymt = _mla(qt, k, vt, km, vtm, gmt, tq=512, tk=1024)

    t, tmeta = _na_tables(na_rel_bias[0], na_meta_bias[0])
    ynt = _na(naq, nak, navt, nakm, navtm, t, tmeta, gnt)

    wo = w_out[0].astype(bf16)
    out = _out_proj(xr, ymt, ynt, wo[:MLA_WIDTH], wo[MLA_WIDTH:], tm=512)
    return out[None]
```

```python
import functools
import math

import numpy as np
import jax
import jax.numpy as jnp
from jax import lax
from jax.experimental import pallas as pl
from jax.experimental.pallas import tpu as pltpu

D_MODEL = 1024
SEQ = 16384
N_META = 16
GRID_W = 64
ROWS = SEQ // GRID_W
EPS = 1e-6

MLA_HEADS = 8
MLA_NOPE = 128
MLA_ROPE = 64
MLA_V = 128
MLA_Q_RANK = 256
MLA_KV_RANK = 256
MLA_WIDTH = MLA_HEADS * MLA_V
ROPE_BASE = 10000.0

NA_HEADS = 16
NA_HEAD_DIM = 64
NA_WIDTH = NA_HEADS * NA_HEAD_DIM
NA_WIN_R = 8
NA_WIN_C = 16

LOG2E = math.log2(math.e)
NEG = -1e30

LANES = 128
MXU_DIM = 256
VMEM_LIMIT_BYTES = 56 * 1024 * 1024

META_PAD = 128
QK_DIM = MXU_DIM
V_ROWS = MLA_V + 16

NA_GROUP = 4
NA_SLAB_ROWS = 10
NA_SLAB = NA_SLAB_ROWS * GRID_W
NA_ROWS_PER_STEP = 8

_NT = (((1,), (1,)), ((), ()))
_TN = (((0,), (0,)), ((), ()))

f32 = jnp.float32
bf16 = jnp.bfloat16


def _rms(x, axis):
    return lax.rsqrt(jnp.mean(x * x, axis=axis, keepdims=True) + EPS)


def _silu(g):
    return g / (1.0 + jnp.exp(-g))


def _proj_kernel(x_ref, cos_ref, sin_ref, nw_ref, wa_ref, wbt_ref, qlw_ref, kvlw_ref,
                 wuqt_ref, wuk_ref, wuvt_ref, qnw_ref, qpew_ref, knw_ref, kpew_ref,
                 naqw_ref, nakw_ref,
                 qt_ref, k_ref, vt_ref, gmt_ref, naq_ref, nak_ref, navt_ref, gnt_ref):
    tm = x_ref.shape[0]
    x = x_ref[...]
    h = (x * _rms(x, 1) * nw_ref[...]).astype(bf16)
    pa = jnp.dot(h, wa_ref[...], preferred_element_type=f32)
    pbt = lax.dot_general(wbt_ref[...], h, _NT, preferred_element_type=f32)

    gmt_ref[...] = pbt[0:MLA_WIDTH].astype(bf16)
    navt_ref[...] = pbt[MLA_WIDTH:MLA_WIDTH + NA_WIDTH].astype(bf16)
    gnt_ref[...] = pbt[MLA_WIDTH + NA_WIDTH:MLA_WIDTH + 2 * NA_WIDTH].astype(bf16)

    cos = cos_ref[...]
    sin = sin_ref[...]
    half = MLA_ROPE // 2

    kpe = pbt[MLA_WIDTH + 2 * NA_WIDTH:MLA_WIDTH + 2 * NA_WIDTH + MLA_ROPE]
    kpe = kpe * _rms(kpe, 0) * kpew_ref[...]
    k1, k2 = kpe[:half], kpe[half:]
    kro = jnp.concatenate(
        [k1 * cos - k2 * sin, k2 * cos + k1 * sin, jnp.zeros((LANES - MLA_ROPE, tm), f32)], axis=0)
    kpe_nat = kro.T.astype(bf16)

    qscale = (MLA_NOPE + MLA_ROPE) ** -0.5 * LOG2E
    ql = pa[:, 0:MLA_Q_RANK]
    qln = (ql * _rms(ql, 1) * qlw_ref[...]).astype(bf16)
    qt = lax.dot_general(wuqt_ref[...], qln, _NT, preferred_element_type=f32)
    per_head = MLA_NOPE + MLA_ROPE
    for hh in range(MLA_HEADS):
        base = per_head * hh
        qn = qt[base:base + MLA_NOPE]
        qn = qn * _rms(qn, 0) * qnw_ref[...]
        qp = qt[base + MLA_NOPE:base + per_head]
        qp = qp * _rms(qp, 0) * qpew_ref[...]
        q1, q2 = qp[:half], qp[half:]
        qt_ref[hh, 0:MLA_NOPE, :] = (qn * qscale).astype(bf16)
        qt_ref[hh, MLA_NOPE:MLA_NOPE + half, :] = ((q1 * cos - q2 * sin) * qscale).astype(bf16)
        qt_ref[hh, MLA_NOPE + half:per_head, :] = ((q2 * cos + q1 * sin) * qscale).astype(bf16)
        qt_ref[hh, per_head:QK_DIM, :] = jnp.zeros((QK_DIM - per_head, tm), bf16)

    kvl = pa[:, MLA_Q_RANK:MLA_Q_RANK + MLA_KV_RANK]
    kvn = (kvl * _rms(kvl, 1) * kvlw_ref[...]).astype(bf16)
    kn = jnp.dot(kvn, wuk_ref[...], preferred_element_type=f32)
    vt = lax.dot_general(wuvt_ref[...], kvn, _NT, preferred_element_type=f32)
    for hh in range(MLA_HEADS):
        xh = kn[:, MLA_NOPE * hh:MLA_NOPE * (hh + 1)]
        xh = xh * _rms(xh, 1) * knw_ref[...]
        k_ref[hh, :, 0:MLA_NOPE] = xh.astype(bf16)
        k_ref[hh, :, MLA_NOPE:QK_DIM] = kpe_nat
        vt_ref[hh, 0:MLA_V, :] = vt[MLA_V * hh:MLA_V * (hh + 1)].astype(bf16)
        vt_ref[hh, MLA_V:V_ROWS, :] = jnp.ones((V_ROWS - MLA_V, tm), bf16)

    lane = lax.broadcasted_iota(jnp.int32, (tm, LANES), 1)
    lo = lane < NA_HEAD_DIM
    na_scale = NA_HEAD_DIM ** -0.5 * LOG2E
    q_off = MLA_Q_RANK + MLA_KV_RANK
    k_off = q_off + NA_WIDTH
    for p in range(NA_WIDTH // LANES):
        for off, w_ref, dst, scale in ((q_off, naqw_ref, naq_ref, na_scale), (k_off, nakw_ref, nak_ref, 1.0)):
            xp = pa[:, off + LANES * p:off + LANES * (p + 1)]
            sq = xp * xp
            s_lo = jnp.sum(jnp.where(lo, sq, 0.0), axis=1, keepdims=True)
            s_hi = jnp.sum(jnp.where(lo, 0.0, sq), axis=1, keepdims=True)
            r = jnp.where(lo, lax.rsqrt(s_lo / NA_HEAD_DIM + EPS), lax.rsqrt(s_hi / NA_HEAD_DIM + EPS))
            y = xp * r * w_ref[...]
            if scale != 1.0:
                y = y * scale
            dst[:, LANES * p:LANES * (p + 1)] = y.astype(bf16)


def _const_spec(shape):
    nd = len(shape)
    return pl.BlockSpec(shape, lambda i, _nd=nd: (0,) * _nd, pipeline_mode=pl.Buffered(1))


def _project(x, cos_t, sin_t, w, tm):
    n = x.shape[0]
    grid = (n // tm,)
    weights = (w["norm_w"], w["wa"], w["wbt"], w["qlw"], w["kvlw"], w["wuqt"], w["wuk"], w["wuvt"],
               w["qnw"], w["qpew"], w["knw"], w["kpew"], w["naqw"], w["nakw"])
    in_specs = [pl.BlockSpec((tm, D_MODEL), lambda i: (i, 0)),
                pl.BlockSpec((MLA_ROPE // 2, tm), lambda i: (0, i)),
                pl.BlockSpec((MLA_ROPE // 2, tm), lambda i: (0, i))]
    in_specs += [_const_spec(a.shape) for a in weights]
    out_shape = (
        jax.ShapeDtypeStruct((MLA_HEADS, QK_DIM, n), bf16),
        jax.ShapeDtypeStruct((MLA_HEADS, n, QK_DIM), bf16),
        jax.ShapeDtypeStruct((MLA_HEADS, V_ROWS, n), bf16),
        jax.ShapeDtypeStruct((MLA_WIDTH, n), bf16),
        jax.ShapeDtypeStruct((n, NA_WIDTH), bf16),
        jax.ShapeDtypeStruct((n, NA_WIDTH), bf16),
        jax.ShapeDtypeStruct((NA_WIDTH, n), bf16),
        jax.ShapeDtypeStruct((NA_WIDTH, n), bf16),
    )
    out_specs = (
        pl.BlockSpec((MLA_HEADS, QK_DIM, tm), lambda i: (0, 0, i)),
        pl.BlockSpec((MLA_HEADS, tm, QK_DIM), lambda i: (0, i, 0)),
        pl.BlockSpec((MLA_HEADS, V_ROWS, tm), lambda i: (0, 0, i)),
        pl.BlockSpec((MLA_WIDTH, tm), lambda i: (0, i)),
        pl.BlockSpec((tm, NA_WIDTH), lambda i: (i, 0)),
        pl.BlockSpec((tm, NA_WIDTH), lambda i: (i, 0)),
        pl.BlockSpec((NA_WIDTH, tm), lambda i: (0, i)),
        pl.BlockSpec((NA_WIDTH, tm), lambda i: (0, i)),
    )
    return pl.pallas_call(
        _proj_kernel,
        grid=grid,
        in_specs=in_specs,
        out_specs=out_specs,
        out_shape=out_shape,
        compiler_params=pltpu.CompilerParams(
            dimension_semantics=("arbitrary",), vmem_limit_bytes=VMEM_LIMIT_BYTES),
        name="proj",
    )(x, cos_t, sin_t, *weights)


def _mla_kernel(qt_ref, k_ref, vt_ref, km_ref, vtm_ref, g_ref, o_ref, acc_ref, m_ref, mx_ref, s_ref, *, tk):
    qt = qt_ref[0]
    nk = k_ref.shape[1] // tk
    assert nk % 2 == 0 and nk >= 2

    def scores(i, slot):
        off = pl.multiple_of(i * tk, tk)
        s = jnp.dot(k_ref[0, pl.ds(off, tk), :], qt, preferred_element_type=f32)
        s_ref[slot] = s
        mx_ref[slot] = jnp.max(s, axis=0, keepdims=True)

    def accumulate(i, slot):
        off = pl.multiple_of(i * tk, tk)
        m_prev = m_ref[...]
        m_new = jnp.maximum(m_prev, mx_ref[slot])
        alpha = jnp.exp2(m_prev - m_new)
        p = jnp.exp2(s_ref[slot] - m_new).astype(bf16)
        pv = jnp.dot(vt_ref[0, :, pl.ds(off, tk)], p, preferred_element_type=f32)
        acc_ref[...] = alpha * acc_ref[...] + pv
        m_ref[...] = m_new

    scores(0, 0)

    s = jnp.dot(km_ref[0], qt, preferred_element_type=f32)
    row = lax.broadcasted_iota(jnp.int32, s.shape, 0)
    s = jnp.where(row < N_META, s, NEG)
    m0 = jnp.max(s, axis=0, keepdims=True)
    p = jnp.exp2(s - m0).astype(bf16)
    acc_ref[...] = jnp.dot(vtm_ref[0], p, preferred_element_type=f32)
    m_ref[...] = m0

    def body(j, carry):
        scores(2 * j + 1, 1)
        accumulate(2 * j, 0)
        scores(2 * j + 2, 0)
        accumulate(2 * j + 1, 1)
        return carry

    lax.fori_loop(0, nk // 2 - 1, body, 0)
    scores(nk - 1, 1)
    accumulate(nk - 2, 0)
    accumulate(nk - 1, 1)

    acc = acc_ref[...]
    o = acc[0:MLA_V] / acc[MLA_V:MLA_V + 1]
    g = g_ref[...].astype(f32)
    o_ref[...] = (o * _silu(g)).astype(bf16)


def _mla(qt, k, vt, km, vtm, gmt, tq, tk):
    n = qt.shape[2]
    grid = (MLA_HEADS, n // tq)
    return pl.pallas_call(
        functools.partial(_mla_kernel, tk=tk),
        grid=grid,
        in_specs=[
            pl.BlockSpec((1, QK_DIM, tq), lambda h, j: (h, 0, j)),
            pl.BlockSpec((1, n, QK_DIM), lambda h, j: (h, 0, 0)),
            pl.BlockSpec((1, V_ROWS, n), lambda h, j: (h, 0, 0)),
            pl.BlockSpec((1, META_PAD, QK_DIM), lambda h, j: (h, 0, 0)),
            pl.BlockSpec((1, V_ROWS, META_PAD), lambda h, j: (h, 0, 0)),
            pl.BlockSpec((MLA_V, tq), lambda h, j: (h, j)),
        ],
        out_specs=pl.BlockSpec((MLA_V, tq), lambda h, j: (h, j)),
        out_shape=jax.ShapeDtypeStruct((MLA_WIDTH, n), bf16),
        scratch_shapes=[pltpu.VMEM((V_ROWS, tq), f32), pltpu.VMEM((1, tq), f32),
                        pltpu.VMEM((2, 1, tq), f32), pltpu.VMEM((2, tk, tq), f32)],
        compiler_params=pltpu.CompilerParams(
            dimension_semantics=("arbitrary", "arbitrary"), vmem_limit_bytes=VMEM_LIMIT_BYTES),
        name="mla",
    )(qt, k, vt, km, vtm, gmt)


NA_DR = 2 * NA_WIN_R - 1
NA_DC = 2 * NA_WIN_C - 1


def _na_row_plan():
    r = np.arange(ROWS)
    r0 = np.clip(r - NA_WIN_R // 2, 0, ROWS - NA_WIN_R)
    r0e = np.minimum(r0 - (r0 % 2), ROWS - NA_SLAB_ROWS)
    kr = r0e[:, None] + np.arange(NA_SLAB_ROWS)[None, :]
    in_win = (kr >= r0[:, None]) & (kr < r0[:, None] + NA_WIN_R)
    dr = kr - r[:, None] + (NA_WIN_R - 1)
    blk = np.where(in_win, dr, NA_DR)
    assert blk.min() >= 0 and blk.max() <= NA_DR
    return r0e.astype(np.int32), blk.reshape(-1).astype(np.int32)


def _na_tables(rel_bias, meta_bias):
    rel = rel_bias.astype(f32)
    half = NA_WIN_C
    w = jnp.concatenate([rel[..., half - 1::-1], jnp.zeros(rel.shape[:2] + (LANES - NA_DC,), f32),
                         rel[..., :half - 1:-1]], axis=-1)
    flat = jnp.tile(w, (1, 1, GRID_W))[..., :GRID_W * (LANES - 1)]
    toe = flat.reshape(NA_HEADS, NA_DR, GRID_W, LANES - 1)[..., :GRID_W]
    c = np.arange(GRID_W)
    c0 = np.clip(c - NA_WIN_C // 2, 0, GRID_W - NA_WIN_C)
    kc = np.arange(GRID_W)
    valid_c = (kc[:, None] >= c0[None, :]) & (kc[:, None] < c0[None, :] + NA_WIN_C)
    t = jnp.where(valid_c[None, None], toe * LOG2E, NEG)
    t = jnp.concatenate([t, jnp.full((NA_HEADS, 1, GRID_W, GRID_W), NEG, f32)], axis=1)
    ng = NA_HEADS // NA_GROUP
    t = t.reshape(ng, NA_GROUP, NA_DR + 1, GRID_W, GRID_W)
    t = t.transpose(0, 2, 3, 1, 4).reshape(ng, NA_DR + 1, GRID_W, NA_GROUP * GRID_W)
    mb = meta_bias.astype(f32) * LOG2E
    mb = jnp.concatenate([mb, jnp.full((NA_HEADS, META_PAD - N_META), NEG, f32)], axis=1)
    tm = jnp.broadcast_to(mb.reshape(ng, NA_GROUP, META_PAD, 1), (ng, NA_GROUP, META_PAD, GRID_W))
    tm = tm.transpose(0, 2, 1, 3).reshape(ng, META_PAD, NA_GROUP * GRID_W)
    return t, tm


def _na_kernel(r0e_ref, blk_ref, q_ref, k_ref, vt_ref, km_ref, vtm_ref, t_ref, tm_ref, g_ref, o_ref):
    b = pl.program_id(1)
    gw = NA_GROUP * GRID_W
    lane_head = lax.broadcasted_iota(jnp.int32, (GRID_W, gw), 1) // GRID_W
    row_head = lax.broadcasted_iota(jnp.int32, (gw, gw), 0) // NA_HEAD_DIM
    col_head = lax.broadcasted_iota(jnp.int32, (gw, gw), 1) // GRID_W
    diag = row_head == col_head
    lane_lo = lax.broadcasted_iota(jnp.int32, (gw, LANES), 1) < GRID_W

    def one_row(r, rl):
        q = q_ref[pl.ds(pl.multiple_of(rl * GRID_W, GRID_W), GRID_W), :].astype(f32)
        wt = jnp.concatenate(
            [jnp.where(lane_head == hp, q, 0.0) for hp in range(NA_GROUP)], axis=0).astype(bf16)
        r0e = r0e_ref[r]
        koff = pl.multiple_of(r0e * GRID_W, LANES)
        ks = k_ref[pl.ds(koff, NA_SLAB), :]
        bias = jnp.concatenate(
            [t_ref[0, blk_ref[r * NA_SLAB_ROWS + j]] for j in range(NA_SLAB_ROWS)], axis=0)
        s = lax.dot_general(ks, wt, _NT, preferred_element_type=f32) + bias
        sm = lax.dot_general(km_ref[...], wt, _NT, preferred_element_type=f32) + tm_ref[0]
        m = jnp.maximum(jnp.max(s, axis=0, keepdims=True), jnp.max(sm, axis=0, keepdims=True))
        p = jnp.exp2(s - m)
        pm = jnp.exp2(sm - m)
        l = jnp.sum(p, axis=0, keepdims=True) + jnp.sum(pm, axis=0, keepdims=True)
        out = jnp.dot(vt_ref[:, pl.ds(koff, NA_SLAB)], p.astype(bf16), preferred_element_type=f32)
        out = out + jnp.dot(vtm_ref[...], pm.astype(bf16), preferred_element_type=f32)
        out = out / l
        x = jnp.where(diag, out, 0.0)
        c = x[:, :LANES] + x[:, LANES:]
        return c + pltpu.roll(c, GRID_W, axis=1)

    def pair(pi, carry):
        r = b * NA_ROWS_PER_STEP + 2 * pi
        d0 = one_row(r, 2 * pi)
        d1 = one_row(r + 1, 2 * pi + 1)
        y = jnp.where(lane_lo, d0, d1)
        loff = pl.multiple_of(pi * LANES, LANES)
        g = g_ref[:, pl.ds(loff, LANES)].astype(f32)
        o_ref[:, pl.ds(loff, LANES)] = (y * _silu(g)).astype(bf16)
        return carry

    lax.fori_loop(0, NA_ROWS_PER_STEP // 2, pair, 0)


def _na(naq, nak, navt, nakm, navtm, t, tmeta, gnt):
    r0e, blk = _na_row_plan()
    ng = NA_HEADS // NA_GROUP
    gw = NA_GROUP * GRID_W
    tq = NA_ROWS_PER_STEP * GRID_W
    grid = (ng, ROWS // NA_ROWS_PER_STEP)
    grid_spec = pltpu.PrefetchScalarGridSpec(
        num_scalar_prefetch=2,
        grid=grid,
        in_specs=[
            pl.BlockSpec((tq, gw), lambda g, b, *_: (b, g)),
            pl.BlockSpec((SEQ, gw), lambda g, b, *_: (0, g)),
            pl.BlockSpec((gw, SEQ), lambda g, b, *_: (g, 0)),
            pl.BlockSpec((META_PAD, gw), lambda g, b, *_: (0, g)),
            pl.BlockSpec((gw, META_PAD), lambda g, b, *_: (g, 0)),
            pl.BlockSpec((1, NA_DR + 1, GRID_W, gw), lambda g, b, *_: (g, 0, 0, 0)),
            pl.BlockSpec((1, META_PAD, gw), lambda g, b, *_: (g, 0, 0)),
            pl.BlockSpec((gw, tq), lambda g, b, *_: (g, b)),
        ],
        out_specs=pl.BlockSpec((gw, tq), lambda g, b, *_: (g, b)),
    )
    return pl.pallas_call(
        _na_kernel,
        grid_spec=grid_spec,
        out_shape=jax.ShapeDtypeStruct((NA_WIDTH, SEQ), bf16),
        compiler_params=pltpu.CompilerParams(
            dimension_semantics=("arbitrary", "arbitrary"), vmem_limit_bytes=VMEM_LIMIT_BYTES),
        name="na",
    )(jnp.asarray(r0e), jnp.asarray(blk), naq, nak, navt, nakm, navtm, t, tmeta, gnt)


def _out_kernel(x_ref, ym_ref, yn_ref, wm_ref, wn_ref, o_ref):
    acc = lax.dot_general(ym_ref[...], wm_ref[...], _TN, preferred_element_type=f32)
    acc = acc + lax.dot_general(yn_ref[...], wn_ref[...], _TN, preferred_element_type=f32)
    o_ref[...] = x_ref[...] + acc


def _out_proj(x, ymt, ynt, wm, wn, tm):
    n = x.shape[0]
    return pl.pallas_call(
        _out_kernel,
        grid=(n // tm,),
        in_specs=[
            pl.BlockSpec((tm, D_MODEL), lambda i: (i, 0)),
            pl.BlockSpec((MLA_WIDTH, tm), lambda i: (0, i)),
            pl.BlockSpec((NA_WIDTH, tm), lambda i: (0, i)),
            _const_spec(wm.shape),
            _const_spec(wn.shape),
        ],
        out_specs=pl.BlockSpec((tm, D_MODEL), lambda i: (i, 0)),
        out_shape=jax.ShapeDtypeStruct((n, D_MODEL), f32),
        compiler_params=pltpu.CompilerParams(
            dimension_semantics=("arbitrary",), vmem_limit_bytes=VMEM_LIMIT_BYTES),
        name="out_proj",
    )(x, ymt, ynt, wm, wn)


def _prep_weights(norm_w, w_in, q_lat_norm_w, kv_lat_norm_w, w_uq, w_ukv,
                  mla_qn_w, mla_qpe_w, mla_kn_w, mla_kpe_w, na_q_norm_w, na_k_norm_w):
    sizes = (MLA_Q_RANK, MLA_KV_RANK, MLA_ROPE, MLA_WIDTH, NA_WIDTH, NA_WIDTH, NA_WIDTH, NA_WIDTH)
    o = np.concatenate([[0], np.cumsum(sizes)])
    seg = [w_in[:, o[i]:o[i + 1]] for i in range(8)]
    wa = jnp.concatenate([seg[0], seg[1], seg[4], seg[5]], axis=1).astype(bf16)
    wbt = jnp.concatenate([seg[3], seg[6], seg[7], seg[2]], axis=1).T.astype(bf16)
    ukv = w_ukv.reshape(MLA_KV_RANK, MLA_HEADS, 2, MLA_NOPE)
    return {
        "norm_w": norm_w.reshape(1, D_MODEL).astype(f32),
        "wa": wa,
        "wbt": wbt,
        "qlw": q_lat_norm_w.reshape(1, MLA_Q_RANK).astype(f32),
        "kvlw": kv_lat_norm_w.reshape(1, MLA_KV_RANK).astype(f32),
        "wuqt": w_uq.T.astype(bf16),
        "wuk": ukv[:, :, 0, :].reshape(MLA_KV_RANK, MLA_HEADS * MLA_NOPE).astype(bf16),
        "wuvt": ukv[:, :, 1, :].reshape(MLA_KV_RANK, MLA_HEADS * MLA_V).T.astype(bf16),
        "qnw": mla_qn_w.reshape(MLA_NOPE, 1).astype(f32),
        "qpew": mla_qpe_w.reshape(MLA_ROPE, 1).astype(f32),
        "knw": mla_kn_w.reshape(1, MLA_NOPE).astype(f32),
        "kpew": mla_kpe_w.reshape(MLA_ROPE, 1).astype(f32),
        "naqw": jnp.tile(na_q_norm_w.reshape(1, NA_HEAD_DIM), (1, 2)).astype(f32),
        "nakw": jnp.tile(na_k_norm_w.reshape(1, NA_HEAD_DIM), (1, 2)).astype(f32),
    }


def kernel(x, meta_tokens, norm_w, w_in, q_lat_norm_w, kv_lat_norm_w, w_uq, w_ukv,
           mla_qn_w, mla_qpe_w, mla_kn_w, mla_kpe_w, na_q_norm_w, na_k_norm_w,
           na_rel_bias, na_meta_bias, w_out):
    assert x.shape == (1, SEQ, D_MODEL) and norm_w.shape[0] == 1
    xr = x[0]
    w = _prep_weights(norm_w[0], w_in[0], q_lat_norm_w[0], kv_lat_norm_w[0], w_uq[0], w_ukv[0],
                      mla_qn_w[0], mla_qpe_w[0], mla_kn_w[0], mla_kpe_w[0],
                      na_q_norm_w[0], na_k_norm_w[0])

    pos = jnp.arange(N_META + SEQ, dtype=f32)
    inv_freq = ROPE_BASE ** (-(jnp.arange(0, MLA_ROPE, 2, dtype=f32) / MLA_ROPE))
    ang = pos[:, None] * inv_freq[None, :]
    cos_t, sin_t = jnp.cos(ang).T, jnp.sin(ang).T
    pad = ((0, 0), (0, META_PAD - N_META))
    cos_m, sin_m = jnp.pad(cos_t[:, :N_META], pad), jnp.pad(sin_t[:, :N_META], pad)

    xm = jnp.pad(meta_tokens.astype(f32), ((0, META_PAD - N_META), (0, 0)))

    qt, k, vt, gmt, naq, nak, navt, gnt = _project(xr, cos_t[:, N_META:], sin_t[:, N_META:], w, tm=256)
    _, km, vtm, _, _, nakm, navtm, _ = _project(xm, cos_m, sin_m, w, tm=META_PAD)

    ymt = _mla(qt, k, vt, km, vtm, gmt, tq=512, tk=1024)

    t, tmeta = _na_tables(na_rel_bias[0], na_meta_bias[0])
    ynt = _na(naq, nak, navt, nakm, navtm, t, tmeta, gnt)

    wo = w_out[0].astype(bf16)
    out = _out_proj(xr, ymt, ynt, wo[:MLA_WIDTH], wo[MLA_WIDTH:], tm=512)
    return out[None]
```

```python
import functools
import math

import numpy as np
import jax
import jax.numpy as jnp
from jax import lax
from jax.experimental import pallas as pl
from jax.experimental.pallas import tpu as pltpu

D_MODEL = 1024
SEQ = 16384
N_META = 16
GRID_W = 64
ROWS = SEQ // GRID_W
EPS = 1e-6

MLA_HEADS = 8
MLA_NOPE = 128
MLA_ROPE = 64
MLA_V = 128
MLA_Q_RANK = 256
MLA_KV_RANK = 256
MLA_WIDTH = MLA_HEADS * MLA_V
ROPE_BASE = 10000.0

NA_HEADS = 16
NA_HEAD_DIM = 64
NA_WIDTH = NA_HEADS * NA_HEAD_DIM
NA_WIN_R = 8
NA_WIN_C = 16

LOG2E = math.log2(math.e)
NEG = -1e30

LANES = 128
MXU_DIM = 256
VMEM_LIMIT_BYTES = 56 * 1024 * 1024

META_PAD = 128
QK_DIM = MXU_DIM
V_ROWS = MLA_V + 16

NA_GROUP = 4
NA_SLAB_ROWS = 10
NA_SLAB = NA_SLAB_ROWS * GRID_W
NA_ROWS_PER_STEP = 32
NA_PAIRS_PER_TRIP = 8
NA_KEYS = NA_SLAB + META_PAD

_NT = (((1,), (1,)), ((), ()))
_TN = (((0,), (0,)), ((), ()))

f32 = jnp.float32
bf16 = jnp.bfloat16


def _rms(x, axis):
    return lax.rsqrt(jnp.mean(x * x, axis=axis, keepdims=True) + EPS)


def _silu(g):
    return g / (1.0 + jnp.exp(-g))


def _proj_kernel(x_ref, cos_ref, sin_ref, nw_ref, wa_ref, wbt_ref, qlw_ref, kvlw_ref,
                 wuqt_ref, wuk_ref, wuvt_ref, qnw_ref, qpew_ref, knw_ref, kpew_ref,
                 naqw_ref, nakw_ref,
                 qt_ref, k_ref, vt_ref, gmt_ref, naq_ref, nak_ref, navt_ref, gnt_ref):
    tm = x_ref.shape[0]
    x = x_ref[...]
    h = (x * _rms(x, 1) * nw_ref[...]).astype(bf16)
    pa = jnp.dot(h, wa_ref[...], preferred_element_type=f32)
    pbt = lax.dot_general(wbt_ref[...], h, _NT, preferred_element_type=f32)

    gmt_ref[...] = pbt[0:MLA_WIDTH].astype(bf16)
    navt_ref[...] = pbt[MLA_WIDTH:MLA_WIDTH + NA_WIDTH].astype(bf16)
    gnt_ref[...] = pbt[MLA_WIDTH + NA_WIDTH:MLA_WIDTH + 2 * NA_WIDTH].astype(bf16)

    cos = cos_ref[...]
    sin = sin_ref[...]
    half = MLA_ROPE // 2

    kpe = pbt[MLA_WIDTH + 2 * NA_WIDTH:MLA_WIDTH + 2 * NA_WIDTH + MLA_ROPE]
    kpe = kpe * _rms(kpe, 0) * kpew_ref[...]
    k1, k2 = kpe[:half], kpe[half:]
    kro = jnp.concatenate(
        [k1 * cos - k2 * sin, k2 * cos + k1 * sin, jnp.zeros((LANES - MLA_ROPE, tm), f32)], axis=0)
    kpe_nat = kro.T.astype(bf16)

    qscale = (MLA_NOPE + MLA_ROPE) ** -0.5 * LOG2E
    ql = pa[:, 0:MLA_Q_RANK]
    qln = (ql * _rms(ql, 1) * qlw_ref[...]).astype(bf16)
    qt = lax.dot_general(wuqt_ref[...], qln, _NT, preferred_element_type=f32)
    per_head = MLA_NOPE + MLA_ROPE
    for hh in range(MLA_HEADS):
        base = per_head * hh
        qn = qt[base:base + MLA_NOPE]
        qn = qn * _rms(qn, 0) * qnw_ref[...]
        qp = qt[base + MLA_NOPE:base + per_head]
        qp = qp * _rms(qp, 0) * qpew_ref[...]
        q1, q2 = qp[:half], qp[half:]
        qt_ref[hh, 0:MLA_NOPE, :] = (qn * qscale).astype(bf16)
        qt_ref[hh, MLA_NOPE:MLA_NOPE + half, :] = ((q1 * cos - q2 * sin) * qscale).astype(bf16)
        qt_ref[hh, MLA_NOPE + half:per_head, :] = ((q2 * cos + q1 * sin) * qscale).astype(bf16)
        qt_ref[hh, per_head:QK_DIM, :] = jnp.zeros((QK_DIM - per_head, tm), bf16)

    kvl = pa[:, MLA_Q_RANK:MLA_Q_RANK + MLA_KV_RANK]
    kvn = (kvl * _rms(kvl, 1) * kvlw_ref[...]).astype(bf16)
    kn = jnp.dot(kvn, wuk_ref[...], preferred_element_type=f32)
    vt = lax.dot_general(wuvt_ref[...], kvn, _NT, preferred_element_type=f32)
    for hh in range(MLA_HEADS):
        xh = kn[:, MLA_NOPE * hh:MLA_NOPE * (hh + 1)]
        xh = xh * _rms(xh, 1) * knw_ref[...]
        k_ref[hh, :, 0:MLA_NOPE] = xh.astype(bf16)
        k_ref[hh, :, MLA_NOPE:QK_DIM] = kpe_nat
        vt_ref[hh, 0:MLA_V, :] = vt[MLA_V * hh:MLA_V * (hh + 1)].astype(bf16)
        vt_ref[hh, MLA_V:V_ROWS, :] = jnp.ones((V_ROWS - MLA_V, tm), bf16)

    lane = lax.broadcasted_iota(jnp.int32, (tm, LANES), 1)
    lo = lane < NA_HEAD_DIM
    na_scale = NA_HEAD_DIM ** -0.5 * LOG2E
    q_off = MLA_Q_RANK + MLA_KV_RANK
    k_off = q_off + NA_WIDTH
    for p in range(NA_WIDTH // LANES):
        for off, w_ref, dst, scale in ((q_off, naqw_ref, naq_ref, na_scale), (k_off, nakw_ref, nak_ref, 1.0)):
            xp = pa[:, off + LANES * p:off + LANES * (p + 1)]
            sq = xp * xp
            s_lo = jnp.sum(jnp.where(lo, sq, 0.0), axis=1, keepdims=True)
            s_hi = jnp.sum(jnp.where(lo, 0.0, sq), axis=1, keepdims=True)
            r = jnp.where(lo, lax.rsqrt(s_lo / NA_HEAD_DIM + EPS), lax.rsqrt(s_hi / NA_HEAD_DIM + EPS))
            y = xp * r * w_ref[...]
            if scale != 1.0:
                y = y * scale
            dst[:, LANES * p:LANES * (p + 1)] = y.astype(bf16)


def _const_spec(shape):
    nd = len(shape)
    return pl.BlockSpec(shape, lambda i, _nd=nd: (0,) * _nd, pipeline_mode=pl.Buffered(1))


def _project(x, cos_t, sin_t, w, tm):
    n = x.shape[0]
    grid = (n // tm,)
    weights = (w["norm_w"], w["wa"], w["wbt"], w["qlw"], w["kvlw"], w["wuqt"], w["wuk"], w["wuvt"],
               w["qnw"], w["qpew"], w["knw"], w["kpew"], w["naqw"], w["nakw"])
    in_specs = [pl.BlockSpec((tm, D_MODEL), lambda i: (i, 0)),
                pl.BlockSpec((MLA_ROPE // 2, tm), lambda i: (0, i)),
                pl.BlockSpec((MLA_ROPE // 2, tm), lambda i: (0, i))]
    in_specs += [_const_spec(a.shape) for a in weights]
    out_shape = (
        jax.ShapeDtypeStruct((MLA_HEADS, QK_DIM, n), bf16),
        jax.ShapeDtypeStruct((MLA_HEADS, n, QK_DIM), bf16),
        jax.ShapeDtypeStruct((MLA_HEADS, V_ROWS, n), bf16),
        jax.ShapeDtypeStruct((MLA_WIDTH, n), bf16),
        jax.ShapeDtypeStruct((n, NA_WIDTH), bf16),
        jax.ShapeDtypeStruct((n, NA_WIDTH), bf16),
        jax.ShapeDtypeStruct((NA_WIDTH, n), bf16),
        jax.ShapeDtypeStruct((NA_WIDTH, n), bf16),
    )
    out_specs = (
        pl.BlockSpec((MLA_HEADS, QK_DIM, tm), lambda i: (0, 0, i)),
        pl.BlockSpec((MLA_HEADS, tm, QK_DIM), lambda i: (0, i, 0)),
        pl.BlockSpec((MLA_HEADS, V_ROWS, tm), lambda i: (0, 0, i)),
        pl.BlockSpec((MLA_WIDTH, tm), lambda i: (0, i)),
        pl.BlockSpec((tm, NA_WIDTH), lambda i: (i, 0)),
        pl.BlockSpec((tm, NA_WIDTH), lambda i: (i, 0)),
        pl.BlockSpec((NA_WIDTH, tm), lambda i: (0, i)),
        pl.BlockSpec((NA_WIDTH, tm), lambda i: (0, i)),
    )
    return pl.pallas_call(
        _proj_kernel,
        grid=grid,
        in_specs=in_specs,
        out_specs=out_specs,
        out_shape=out_shape,
        compiler_params=pltpu.CompilerParams(
            dimension_semantics=("arbitrary",), vmem_limit_bytes=VMEM_LIMIT_BYTES),
        name="proj",
    )(x, cos_t, sin_t, *weights)


def _mla_kernel(qt_ref, k_ref, vt_ref, km_ref, vtm_ref, g_ref, o_ref, acc_ref, m_ref, mx_ref, s_ref, *, tk):
    qt = qt_ref[0]
    nk = k_ref.shape[1] // tk
    assert nk % 2 == 0 and nk >= 2

    def scores(i, slot):
        off = pl.multiple_of(i * tk, tk)
        s = jnp.dot(k_ref[0, pl.ds(off, tk), :], qt, preferred_element_type=f32)
        s_ref[slot] = s
        mx_ref[slot] = jnp.max(s, axis=0, keepdims=True)

    def accumulate(i, slot):
        off = pl.multiple_of(i * tk, tk)
        m_prev = m_ref[...]
        m_new = jnp.maximum(m_prev, mx_ref[slot])
        alpha = jnp.exp2(m_prev - m_new)
        p = jnp.exp2(s_ref[slot] - m_new).astype(bf16)
        pv = jnp.dot(vt_ref[0, :, pl.ds(off, tk)], p, preferred_element_type=f32)
        acc_ref[...] = alpha * acc_ref[...] + pv
        m_ref[...] = m_new

    scores(0, 0)

    s = jnp.dot(km_ref[0], qt, preferred_element_type=f32)
    row = lax.broadcasted_iota(jnp.int32, s.shape, 0)
    s = jnp.where(row < N_META, s, NEG)
    m0 = jnp.max(s, axis=0, keepdims=True)
    p = jnp.exp2(s - m0).astype(bf16)
    acc_ref[...] = jnp.dot(vtm_ref[0], p, preferred_element_type=f32)
    m_ref[...] = m0

    def body(j, carry):
        scores(2 * j + 1, 1)
        accumulate(2 * j, 0)
        scores(2 * j + 2, 0)
        accumulate(2 * j + 1, 1)
        return carry

    lax.fori_loop(0, nk // 2 - 1, body, 0)
    scores(nk - 1, 1)
    accumulate(nk - 2, 0)
    accumulate(nk - 1, 1)

    acc = acc_ref[...]
    o = acc[0:MLA_V] / acc[MLA_V:MLA_V + 1]
    g = g_ref[...].astype(f32)
    o_ref[...] = (o * _silu(g)).astype(bf16)


def _mla(qt, k, vt, km, vtm, gmt, tq, tk):
    n = qt.shape[2]
    grid = (MLA_HEADS, n // tq)
    return pl.pallas_call(
        functools.partial(_mla_kernel, tk=tk),
        grid=grid,
        in_specs=[
            pl.BlockSpec((1, QK_DIM, tq), lambda h, j: (h, 0, j)),
            pl.BlockSpec((1, n, QK_DIM), lambda h, j: (h, 0, 0)),
            pl.BlockSpec((1, V_ROWS, n), lambda h, j: (h, 0, 0)),
            pl.BlockSpec((1, META_PAD, QK_DIM), lambda h, j: (h, 0, 0)),
            pl.BlockSpec((1, V_ROWS, META_PAD), lambda h, j: (h, 0, 0)),
            pl.BlockSpec((MLA_V, tq), lambda h, j: (h, j)),
        ],
        out_specs=pl.BlockSpec((MLA_V, tq), lambda h, j: (h, j)),
        out_shape=jax.ShapeDtypeStruct((MLA_WIDTH, n), bf16),
        scratch_shapes=[pltpu.VMEM((V_ROWS, tq), f32), pltpu.VMEM((1, tq), f32),
                        pltpu.VMEM((2, 1, tq), f32), pltpu.VMEM((2, tk, tq), f32)],
        compiler_params=pltpu.CompilerParams(
            dimension_semantics=("arbitrary", "arbitrary"), vmem_limit_bytes=VMEM_LIMIT_BYTES),
        name="mla",
    )(qt, k, vt, km, vtm, gmt)


NA_DR = 2 * NA_WIN_R - 1
NA_DC = 2 * NA_WIN_C - 1


def _na_row_plan():
    r = np.arange(ROWS)
    r0 = np.clip(r - NA_WIN_R // 2, 0, ROWS - NA_WIN_R)
    r0e = np.minimum(r0 - (r0 % 2), ROWS - NA_SLAB_ROWS)
    kr = r0e[:, None] + np.arange(NA_SLAB_ROWS)[None, :]
    in_win = (kr >= r0[:, None]) & (kr < r0[:, None] + NA_WIN_R)
    dr = kr - r[:, None] + (NA_WIN_R - 1)
    blk = np.where(in_win, dr, NA_DR)
    assert blk.min() >= 0 and blk.max() <= NA_DR
    return r0e.astype(np.int32), blk.reshape(-1).astype(np.int32)


def _na_tables(rel_bias, meta_bias):
    rel = rel_bias.astype(f32)
    half = NA_WIN_C
    w = jnp.concatenate([rel[..., half - 1::-1], jnp.zeros(rel.shape[:2] + (LANES - NA_DC,), f32),
                         rel[..., :half - 1:-1]], axis=-1)
    flat = jnp.tile(w, (1, 1, GRID_W))[..., :GRID_W * (LANES - 1)]
    toe = flat.reshape(NA_HEADS, NA_DR, GRID_W, LANES - 1)[..., :GRID_W]
    c = np.arange(GRID_W)
    c0 = np.clip(c - NA_WIN_C // 2, 0, GRID_W - NA_WIN_C)
    kc = np.arange(GRID_W)
    valid_c = (kc[:, None] >= c0[None, :]) & (kc[:, None] < c0[None, :] + NA_WIN_C)
    t = jnp.where(valid_c[None, None], toe * LOG2E, NEG)
    t = jnp.concatenate([t, jnp.full((NA_HEADS, 1, GRID_W, GRID_W), NEG, f32)], axis=1)
    ng = NA_HEADS // NA_GROUP
    t = t.reshape(ng, NA_GROUP, NA_DR + 1, GRID_W, GRID_W)
    t = t.transpose(0, 2, 3, 1, 4).reshape(ng, NA_DR + 1, GRID_W, NA_GROUP * GRID_W)
    mb = meta_bias.astype(f32) * LOG2E
    mb = jnp.concatenate([mb, jnp.full((NA_HEADS, META_PAD - N_META), NEG, f32)], axis=1)
    tm = jnp.broadcast_to(mb.reshape(ng, NA_GROUP, META_PAD, 1), (ng, NA_GROUP, META_PAD, GRID_W))
    tm = tm.transpose(0, 2, 1, 3).reshape(ng, META_PAD, NA_GROUP * GRID_W)
    return t, tm


def _na_kernel(r0e_ref, blk_ref, q_ref, k_ref, vt_ref, km_ref, vtm_ref, t_ref, tm_ref, g_ref, wmask_ref,
               o_ref, s_ref, mx_ref):
    b = pl.program_id(1)
    gw = NA_GROUP * GRID_W
    lane_lo = lax.broadcasted_iota(jnp.int32, (gw, LANES), 1) < GRID_W
    head_even = (lax.broadcasted_iota(jnp.int32, (gw, LANES), 0) // NA_HEAD_DIM) % 2 == 0
    own_half = lane_lo == head_even
    last = NA_ROWS_PER_STEP - 1

    def scores(rl, slot):
        r = b * NA_ROWS_PER_STEP + rl
        q = q_ref[pl.ds(pl.multiple_of(rl * GRID_W, GRID_W), GRID_W), :]
        wt = jnp.concatenate([q] * NA_GROUP, axis=0) * wmask_ref[...]
        koff = pl.multiple_of(r0e_ref[r] * GRID_W, LANES)
        keys = jnp.concatenate([k_ref[pl.ds(koff, NA_SLAB), :], km_ref[...]], axis=0)
        bias = jnp.concatenate(
            [t_ref[0, blk_ref[r * NA_SLAB_ROWS + j]] for j in range(NA_SLAB_ROWS)] + [tm_ref[0]], axis=0)
        s = lax.dot_general(keys, wt, _NT, preferred_element_type=f32) + bias
        s_ref[slot] = s
        mx_ref[slot] = jnp.max(s, axis=0, keepdims=True)

    def finish(rl, slot):
        r = b * NA_ROWS_PER_STEP + rl
        koff = pl.multiple_of(r0e_ref[r] * GRID_W, LANES)
        p = jnp.exp2(s_ref[slot] - mx_ref[slot])
        l = jnp.sum(p, axis=0, keepdims=True)
        vals = jnp.concatenate([vt_ref[:, pl.ds(koff, NA_SLAB)], vtm_ref[...]], axis=1)
        out = jnp.dot(vals, p.astype(bf16), preferred_element_type=f32)
        rl_ = 1.0 / l
        c = jnp.concatenate([out[:LANES, :LANES] * rl_[:, :LANES], out[LANES:, LANES:] * rl_[:, LANES:]], axis=0)
        return jnp.where(own_half, c, pltpu.roll(c, GRID_W, axis=1))

    scores(0, 0)

    def pair(pi):
        ra = 2 * pi
        scores(ra + 1, 1)
        d0 = finish(ra, 0)
        scores(jnp.minimum(ra + 2, last), 0)
        d1 = finish(ra + 1, 1)
        y = jnp.where(lane_lo, d0, d1)
        loff = pl.multiple_of(pi * LANES, LANES)
        g = g_ref[:, pl.ds(loff, LANES)].astype(f32)
        o_ref[:, pl.ds(loff, LANES)] = (y * _silu(g)).astype(bf16)

    def trip(t, carry):
        for u in range(NA_PAIRS_PER_TRIP):
            pair(t * NA_PAIRS_PER_TRIP + u)
        return carry

    lax.fori_loop(0, NA_ROWS_PER_STEP // (2 * NA_PAIRS_PER_TRIP), trip, 0)


def _na(naq, nak, navt, nakm, navtm, t, tmeta, gnt):
    r0e, blk = _na_row_plan()
    wmask = jnp.asarray(np.kron(np.eye(NA_GROUP), np.ones((GRID_W, NA_HEAD_DIM))), bf16)
    ng = NA_HEADS // NA_GROUP
    gw = NA_GROUP * GRID_W
    tq = NA_ROWS_PER_STEP * GRID_W
    grid = (ng, ROWS // NA_ROWS_PER_STEP)
    grid_spec = pltpu.PrefetchScalarGridSpec(
        num_scalar_prefetch=2,
        grid=grid,
        in_specs=[
            pl.BlockSpec((tq, gw), lambda g, b, *_: (b, g)),
            pl.BlockSpec((SEQ, gw), lambda g, b, *_: (0, g)),
            pl.BlockSpec((gw, SEQ), lambda g, b, *_: (g, 0)),
            pl.BlockSpec((META_PAD, gw), lambda g, b, *_: (0, g)),
            pl.BlockSpec((gw, META_PAD), lambda g, b, *_: (g, 0)),
            pl.BlockSpec((1, NA_DR + 1, GRID_W, gw), lambda g, b, *_: (g, 0, 0, 0)),
            pl.BlockSpec((1, META_PAD, gw), lambda g, b, *_: (g, 0, 0)),
            pl.BlockSpec((gw, tq), lambda g, b, *_: (g, b)),
            pl.BlockSpec((gw, gw), lambda g, b, *_: (0, 0)),
        ],
        out_specs=pl.BlockSpec((gw, tq), lambda g, b, *_: (g, b)),
        scratch_shapes=[pltpu.VMEM((2, NA_KEYS, gw), f32), pltpu.VMEM((2, 1, gw), f32)],
    )
    return pl.pallas_call(
        _na_kernel,
        grid_spec=grid_spec,
        out_shape=jax.ShapeDtypeStruct((NA_WIDTH, SEQ), bf16),
        compiler_params=pltpu.CompilerParams(
            dimension_semantics=("arbitrary", "arbitrary"), vmem_limit_bytes=VMEM_LIMIT_BYTES),
        name="na",
    )(jnp.asarray(r0e), jnp.asarray(blk), naq, nak, navt, nakm, navtm, t, tmeta, gnt, wmask)


def _out_kernel(x_ref, ym_ref, yn_ref, wm_ref, wn_ref, o_ref):
    acc = lax.dot_general(ym_ref[...], wm_ref[...], _TN, preferred_element_type=f32)
    acc = acc + lax.dot_general(yn_ref[...], wn_ref[...], _TN, preferred_element_type=f32)
    o_ref[...] = x_ref[...] + acc


def _out_proj(x, ymt, ynt, wm, wn, tm):
    n = x.shape[0]
    return pl.pallas_call(
        _out_kernel,
        grid=(n // tm,),
        in_specs=[
            pl.BlockSpec((tm, D_MODEL), lambda i: (i, 0)),
            pl.BlockSpec((MLA_WIDTH, tm), lambda i: (0, i)),
            pl.BlockSpec((NA_WIDTH, tm), lambda i: (0, i)),
            _const_spec(wm.shape),
            _const_spec(wn.shape),
        ],
        out_specs=pl.BlockSpec((tm, D_MODEL), lambda i: (i, 0)),
        out_shape=jax.ShapeDtypeStruct((n, D_MODEL), f32),
        compiler_params=pltpu.CompilerParams(
            dimension_semantics=("arbitrary",), vmem_limit_bytes=VMEM_LIMIT_BYTES),
        name="out_proj",
    )(x, ymt, ynt, wm, wn)


def _prep_weights(norm_w, w_in, q_lat_norm_w, kv_lat_norm_w, w_uq, w_ukv,
                  mla_qn_w, mla_qpe_w, mla_kn_w, mla_kpe_w, na_q_norm_w, na_k_norm_w):
    sizes = (MLA_Q_RANK, MLA_KV_RANK, MLA_ROPE, MLA_WIDTH, NA_WIDTH, NA_WIDTH, NA_WIDTH, NA_WIDTH)
    o = np.concatenate([[0], np.cumsum(sizes)])
    seg = [w_in[:, o[i]:o[i + 1]] for i in range(8)]
    wa = jnp.concatenate([seg[0], seg[1], seg[4], seg[5]], axis=1).astype(bf16)
    wbt = jnp.concatenate([seg[3], seg[6], seg[7], seg[2]], axis=1).T.astype(bf16)
    ukv = w_ukv.reshape(MLA_KV_RANK, MLA_HEADS, 2, MLA_NOPE)
    return {
        "norm_w": norm_w.reshape(1, D_MODEL).astype(f32),
        "wa": wa,
        "wbt": wbt,
        "qlw": q_lat_norm_w.reshape(1, MLA_Q_RANK).astype(f32),
        "kvlw": kv_lat_norm_w.reshape(1, MLA_KV_RANK).astype(f32),
        "wuqt": w_uq.T.astype(bf16),
        "wuk": ukv[:, :, 0, :].reshape(MLA_KV_RANK, MLA_HEADS * MLA_NOPE).astype(bf16),
        "wuvt": ukv[:, :, 1, :].reshape(MLA_KV_RANK, MLA_HEADS * MLA_V).T.astype(bf16),
        "qnw": mla_qn_w.reshape(MLA_NOPE, 1).astype(f32),
        "qpew": mla_qpe_w.reshape(MLA_ROPE, 1).astype(f32),
        "knw": mla_kn_w.reshape(1, MLA_NOPE).astype(f32),
        "kpew": mla_kpe_w.reshape(MLA_ROPE, 1).astype(f32),
        "naqw": jnp.tile(na_q_norm_w.reshape(1, NA_HEAD_DIM), (1, 2)).astype(f32),
        "nakw": jnp.tile(na_k_norm_w.reshape(1, NA_HEAD_DIM), (1, 2)).astype(f32),
    }


def kernel(x, meta_tokens, norm_w, w_in, q_lat_norm_w, kv_lat_norm_w, w_uq, w_ukv,
           mla_qn_w, mla_qpe_w, mla_kn_w, mla_kpe_w, na_q_norm_w, na_k_norm_w,
           na_rel_bias, na_meta_bias, w_out):
    assert x.shape == (1, SEQ, D_MODEL) and norm_w.shape[0] == 1
    xr = x[0]
    w = _prep_weights(norm_w[0], w_in[0], q_lat_norm_w[0], kv_lat_norm_w[0], w_uq[0], w_ukv[0],
                      mla_qn_w[0], mla_qpe_w[0], mla_kn_w[0], mla_kpe_w[0],
                      na_q_norm_w[0], na_k_norm_w[0])

    pos = jnp.arange(N_META + SEQ, dtype=f32)
    inv_freq = ROPE_BASE ** (-(jnp.arange(0, MLA_ROPE, 2, dtype=f32) / MLA_ROPE))
    ang = pos[:, None] * inv_freq[None, :]
    cos_t, sin_t = jnp.cos(ang).T, jnp.sin(ang).T
    pad = ((0, 0), (0, META_PAD - N_META))
    cos_m, sin_m = jnp.pad(cos_t[:, :N_META], pad), jnp.pad(sin_t[:, :N_META], pad)

    xm = jnp.pad(meta_tokens.astype(f32), ((0, META_PAD - N_META), (0, 0)))

    qt, k, vt, gmt, naq, nak, navt, gnt = _project(xr, cos_t[:, N_META:], sin_t[:, N_META:], w, tm=256)
    _, km, vtm, _, _, nakm, navtm, _ = _project(xm, cos_m, sin_m, w, tm=META_PAD)

    ymt = _mla(qt, k, vt, km, vtm, gmt, tq=1024, tk=1024)

    t, tmeta = _na_tables(na_rel_bias[0], na_meta_bias[0])
    ynt = _na(naq, nak, navt, nakm, navtm, t, tmeta, gnt)

    wo = w_out[0].astype(bf16)
    out = _out_proj(xr, ymt, ynt, wo[:MLA_WIDTH], wo[MLA_WIDTH:], tm=512)
    return out[None]
```

```python
import functools
import math

import numpy as np
import jax
import jax.numpy as jnp
from jax import lax
from jax.experimental import pallas as pl
from jax.experimental.pallas import tpu as pltpu

D_MODEL = 1024
SEQ = 16384
N_META = 16
GRID_W = 64
ROWS = SEQ // GRID_W
EPS = 1e-6

MLA_HEADS = 8
MLA_NOPE = 128
MLA_ROPE = 64
MLA_V = 128
MLA_Q_RANK = 256
MLA_KV_RANK = 256
MLA_WIDTH = MLA_HEADS * MLA_V
ROPE_BASE = 10000.0

NA_HEADS = 16
NA_HEAD_DIM = 64
NA_WIDTH = NA_HEADS * NA_HEAD_DIM
NA_WIN_R = 8
NA_WIN_C = 16

LOG2E = math.log2(math.e)
NEG = -1e30

LANES = 128
MXU_DIM = 256
VMEM_LIMIT_BYTES = 56 * 1024 * 1024

META_PAD = 128
QK_DIM = MXU_DIM
V_ROWS = MLA_V + 16

NA_GROUP = 4
NA_SLAB_ROWS = 10
NA_SLAB = NA_SLAB_ROWS * GRID_W
NA_ROWS_PER_STEP = 32
NA_PAIRS_PER_TRIP = 8
NA_KEYS = NA_SLAB + META_PAD

_NT = (((1,), (1,)), ((), ()))
_TN = (((0,), (0,)), ((), ()))

f32 = jnp.float32
bf16 = jnp.bfloat16

MLA_QK_DTYPE = jnp.float8_e4m3fn
MLA_QK_SCALE = math.sqrt((MLA_NOPE + MLA_ROPE) ** -0.5 * LOG2E)


def _rms(x, axis):
    return lax.rsqrt(jnp.mean(x * x, axis=axis, keepdims=True) + EPS)


def _silu(g):
    return g / (1.0 + jnp.exp(-g))


def _proj_kernel(x_ref, cos_ref, sin_ref, nw_ref, wa_ref, wbt_ref, qlw_ref, kvlw_ref,
                 wuqt_ref, wuk_ref, wuvt_ref, qnw_ref, qpew_ref, knw_ref, kpew_ref,
                 naqw_ref, nakw_ref,
                 qt_ref, k_ref, vt_ref, gmt_ref, naq_ref, nak_ref, navt_ref, gnt_ref):
    tm = x_ref.shape[0]
    x = x_ref[...]
    h = (x * _rms(x, 1) * nw_ref[...]).astype(bf16)
    pa = jnp.dot(h, wa_ref[...], preferred_element_type=f32)
    pbt = lax.dot_general(wbt_ref[...], h, _NT, preferred_element_type=f32)

    gmt_ref[...] = pbt[0:MLA_WIDTH].astype(bf16)
    navt_ref[...] = pbt[MLA_WIDTH:MLA_WIDTH + NA_WIDTH].astype(bf16)
    gnt_ref[...] = pbt[MLA_WIDTH + NA_WIDTH:MLA_WIDTH + 2 * NA_WIDTH].astype(bf16)

    cos = cos_ref[...]
    sin = sin_ref[...]
    half = MLA_ROPE // 2

    kpe = pbt[MLA_WIDTH + 2 * NA_WIDTH:MLA_WIDTH + 2 * NA_WIDTH + MLA_ROPE]
    kpe = kpe * _rms(kpe, 0) * kpew_ref[...]
    k1, k2 = kpe[:half], kpe[half:]
    kro = jnp.concatenate(
        [k1 * cos - k2 * sin, k2 * cos + k1 * sin, jnp.zeros((LANES - MLA_ROPE, tm), f32)], axis=0)
    kpe_nat = (kro.T * MLA_QK_SCALE).astype(MLA_QK_DTYPE)

    qscale = MLA_QK_SCALE
    ql = pa[:, 0:MLA_Q_RANK]
    qln = (ql * _rms(ql, 1) * qlw_ref[...]).astype(bf16)
    qt = lax.dot_general(wuqt_ref[...], qln, _NT, preferred_element_type=f32)
    per_head = MLA_NOPE + MLA_ROPE
    for hh in range(MLA_HEADS):
        base = per_head * hh
        qn = qt[base:base + MLA_NOPE]
        qn = qn * _rms(qn, 0) * qnw_ref[...]
        qp = qt[base + MLA_NOPE:base + per_head]
        qp = qp * _rms(qp, 0) * qpew_ref[...]
        q1, q2 = qp[:half], qp[half:]
        qt_ref[hh, 0:MLA_NOPE, :] = (qn * qscale).astype(MLA_QK_DTYPE)
        qt_ref[hh, MLA_NOPE:MLA_NOPE + half, :] = ((q1 * cos - q2 * sin) * qscale).astype(MLA_QK_DTYPE)
        qt_ref[hh, MLA_NOPE + half:per_head, :] = ((q2 * cos + q1 * sin) * qscale).astype(MLA_QK_DTYPE)
        qt_ref[hh, per_head:QK_DIM, :] = jnp.zeros((QK_DIM - per_head, tm), MLA_QK_DTYPE)

    kvl = pa[:, MLA_Q_RANK:MLA_Q_RANK + MLA_KV_RANK]
    kvn = (kvl * _rms(kvl, 1) * kvlw_ref[...]).astype(bf16)
    kn = jnp.dot(kvn, wuk_ref[...], preferred_element_type=f32)
    vt = lax.dot_general(wuvt_ref[...], kvn, _NT, preferred_element_type=f32)
    for hh in range(MLA_HEADS):
        xh = kn[:, MLA_NOPE * hh:MLA_NOPE * (hh + 1)]
        xh = xh * _rms(xh, 1) * knw_ref[...]
        k_ref[hh, :, 0:MLA_NOPE] = (xh * MLA_QK_SCALE).astype(MLA_QK_DTYPE)
        k_ref[hh, :, MLA_NOPE:QK_DIM] = kpe_nat
        vt_ref[hh, 0:MLA_V, :] = vt[MLA_V * hh:MLA_V * (hh + 1)].astype(bf16)
        vt_ref[hh, MLA_V:V_ROWS, :] = jnp.ones((V_ROWS - MLA_V, tm), bf16)

    lane = lax.broadcasted_iota(jnp.int32, (tm, LANES), 1)
    lo = lane < NA_HEAD_DIM
    na_scale = NA_HEAD_DIM ** -0.5 * LOG2E
    q_off = MLA_Q_RANK + MLA_KV_RANK
    k_off = q_off + NA_WIDTH
    for p in range(NA_WIDTH // LANES):
        for off, w_ref, dst, scale in ((q_off, naqw_ref, naq_ref, na_scale), (k_off, nakw_ref, nak_ref, 1.0)):
            xp = pa[:, off + LANES * p:off + LANES * (p + 1)]
            sq = xp * xp
            s_lo = jnp.sum(jnp.where(lo, sq, 0.0), axis=1, keepdims=True)
            s_hi = jnp.sum(jnp.where(lo, 0.0, sq), axis=1, keepdims=True)
            r = jnp.where(lo, lax.rsqrt(s_lo / NA_HEAD_DIM + EPS), lax.rsqrt(s_hi / NA_HEAD_DIM + EPS))
            y = xp * r * w_ref[...]
            if scale != 1.0:
                y = y * scale
            dst[:, LANES * p:LANES * (p + 1)] = y.astype(bf16)


def _const_spec(shape):
    nd = len(shape)
    return pl.BlockSpec(shape, lambda i, _nd=nd: (0,) * _nd, pipeline_mode=pl.Buffered(1))


def _project(x, cos_t, sin_t, w, tm):
    n = x.shape[0]
    grid = (n // tm,)
    weights = (w["norm_w"], w["wa"], w["wbt"], w["qlw"], w["kvlw"], w["wuqt"], w["wuk"], w["wuvt"],
               w["qnw"], w["qpew"], w["knw"], w["kpew"], w["naqw"], w["nakw"])
    in_specs = [pl.BlockSpec((tm, D_MODEL), lambda i: (i, 0)),
                pl.BlockSpec((MLA_ROPE // 2, tm), lambda i: (0, i)),
                pl.BlockSpec((MLA_ROPE // 2, tm), lambda i: (0, i))]
    in_specs += [_const_spec(a.shape) for a in weights]
    out_shape = (
        jax.ShapeDtypeStruct((MLA_HEADS, QK_DIM, n), MLA_QK_DTYPE),
        jax.ShapeDtypeStruct((MLA_HEADS, n, QK_DIM), MLA_QK_DTYPE),
        jax.ShapeDtypeStruct((MLA_HEADS, V_ROWS, n), bf16),
        jax.ShapeDtypeStruct((MLA_WIDTH, n), bf16),
        jax.ShapeDtypeStruct((n, NA_WIDTH), bf16),
        jax.ShapeDtypeStruct((n, NA_WIDTH), bf16),
        jax.ShapeDtypeStruct((NA_WIDTH, n), bf16),
        jax.ShapeDtypeStruct((NA_WIDTH, n), bf16),
    )
    out_specs = (
        pl.BlockSpec((MLA_HEADS, QK_DIM, tm), lambda i: (0, 0, i)),
        pl.BlockSpec((MLA_HEADS, tm, QK_DIM), lambda i: (0, i, 0)),
        pl.BlockSpec((MLA_HEADS, V_ROWS, tm), lambda i: (0, 0, i)),
        pl.BlockSpec((MLA_WIDTH, tm), lambda i: (0, i)),
        pl.BlockSpec((tm, NA_WIDTH), lambda i: (i, 0)),
        pl.BlockSpec((tm, NA_WIDTH), lambda i: (i, 0)),
        pl.BlockSpec((NA_WIDTH, tm), lambda i: (0, i)),
        pl.BlockSpec((NA_WIDTH, tm), lambda i: (0, i)),
    )
    return pl.pallas_call(
        _proj_kernel,
        grid=grid,
        in_specs=in_specs,
        out_specs=out_specs,
        out_shape=out_shape,
        compiler_params=pltpu.CompilerParams(
            dimension_semantics=("arbitrary",), vmem_limit_bytes=VMEM_LIMIT_BYTES),
        name="proj",
    )(x, cos_t, sin_t, *weights)


def _mla_kernel(qt_ref, k_ref, vt_ref, km_ref, vtm_ref, g_ref, o_ref, acc_ref, m_ref, mx_ref, s_ref, *, tk):
    qt = qt_ref[0]
    nk = k_ref.shape[1] // tk
    assert nk % 2 == 0 and nk >= 2

    def scores(i, slot):
        off = pl.multiple_of(i * tk, tk)
        s = jnp.dot(k_ref[0, pl.ds(off, tk), :], qt, preferred_element_type=f32)
        s_ref[slot] = s
        mx_ref[slot] = jnp.max(s, axis=0, keepdims=True)

    def accumulate(i, slot):
        off = pl.multiple_of(i * tk, tk)
        m_prev = m_ref[...]
        m_new = jnp.maximum(m_prev, mx_ref[slot])
        alpha = jnp.exp2(m_prev - m_new)
        p = jnp.exp2(s_ref[slot] - m_new).astype(bf16)
        pv = jnp.dot(vt_ref[0, :, pl.ds(off, tk)], p, preferred_element_type=f32)
        acc_ref[...] = alpha * acc_ref[...] + pv
        m_ref[...] = m_new

    scores(0, 0)

    s = jnp.dot(km_ref[0], qt, preferred_element_type=f32)
    row = lax.broadcasted_iota(jnp.int32, s.shape, 0)
    s = jnp.where(row < N_META, s, NEG)
    m0 = jnp.max(s, axis=0, keepdims=True)
    p = jnp.exp2(s - m0).astype(bf16)
    acc_ref[...] = jnp.dot(vtm_ref[0], p, preferred_element_type=f32)
    m_ref[...] = m0

    def body(j, carry):
        scores(2 * j + 1, 1)
        accumulate(2 * j, 0)
        scores(2 * j + 2, 0)
        accumulate(2 * j + 1, 1)
        return carry

    lax.fori_loop(0, nk // 2 - 1, body, 0)
    scores(nk - 1, 1)
    accumulate(nk - 2, 0)
    accumulate(nk - 1, 1)

    acc = acc_ref[...]
    o = acc[0:MLA_V] / acc[MLA_V:MLA_V + 1]
    g = g_ref[...].astype(f32)
    o_ref[...] = (o * _silu(g)).astype(bf16)


def _mla(qt, k, vt, km, vtm, gmt, tq, tk):
    n = qt.shape[2]
    grid = (MLA_HEADS, n // tq)
    return pl.pallas_call(
        functools.partial(_mla_kernel, tk=tk),
        grid=grid,
        in_specs=[
            pl.BlockSpec((1, QK_DIM, tq), lambda h, j: (h, 0, j)),
            pl.BlockSpec((1, n, QK_DIM), lambda h, j: (h, 0, 0)),
            pl.BlockSpec((1, V_ROWS, n), lambda h, j: (h, 0, 0)),
            pl.BlockSpec((1, META_PAD, QK_DIM), lambda h, j: (h, 0, 0)),
            pl.BlockSpec((1, V_ROWS, META_PAD), lambda h, j: (h, 0, 0)),
            pl.BlockSpec((MLA_V, tq), lambda h, j: (h, j)),
        ],
        out_specs=pl.BlockSpec((MLA_V, tq), lambda h, j: (h, j)),
        out_shape=jax.ShapeDtypeStruct((MLA_WIDTH, n), bf16),
        scratch_shapes=[pltpu.VMEM((V_ROWS, tq), f32), pltpu.VMEM((1, tq), f32),
                        pltpu.VMEM((2, 1, tq), f32), pltpu.VMEM((2, tk, tq), f32)],
        compiler_params=pltpu.CompilerParams(
            dimension_semantics=("arbitrary", "arbitrary"), vmem_limit_bytes=VMEM_LIMIT_BYTES),
        name="mla",
    )(qt, k, vt, km, vtm, gmt)


NA_DR = 2 * NA_WIN_R - 1
NA_DC = 2 * NA_WIN_C - 1


def _na_row_plan():
    r = np.arange(ROWS)
    r0 = np.clip(r - NA_WIN_R // 2, 0, ROWS - NA_WIN_R)
    r0e = np.minimum(r0 - (r0 % 2), ROWS - NA_SLAB_ROWS)
    kr = r0e[:, None] + np.arange(NA_SLAB_ROWS)[None, :]
    in_win = (kr >= r0[:, None]) & (kr < r0[:, None] + NA_WIN_R)
    dr = kr - r[:, None] + (NA_WIN_R - 1)
    blk = np.where(in_win, dr, NA_DR)
    assert blk.min() >= 0 and blk.max() <= NA_DR
    return r0e.astype(np.int32), blk.reshape(-1).astype(np.int32)


def _na_tables(rel_bias, meta_bias):
    rel = rel_bias.astype(f32)
    half = NA_WIN_C
    w = jnp.concatenate([rel[..., half - 1::-1], jnp.zeros(rel.shape[:2] + (LANES - NA_DC,), f32),
                         rel[..., :half - 1:-1]], axis=-1)
    flat = jnp.tile(w, (1, 1, GRID_W))[..., :GRID_W * (LANES - 1)]
    toe = flat.reshape(NA_HEADS, NA_DR, GRID_W, LANES - 1)[..., :GRID_W]
    c = np.arange(GRID_W)
    c0 = np.clip(c - NA_WIN_C // 2, 0, GRID_W - NA_WIN_C)
    kc = np.arange(GRID_W)
    valid_c = (kc[:, None] >= c0[None, :]) & (kc[:, None] < c0[None, :] + NA_WIN_C)
    t = jnp.where(valid_c[None, None], toe * LOG2E, NEG)
    t = jnp.concatenate([t, jnp.full((NA_HEADS, 1, GRID_W, GRID_W), NEG, f32)], axis=1)
    ng = NA_HEADS // NA_GROUP
    t = t.reshape(ng, NA_GROUP, NA_DR + 1, GRID_W, GRID_W)
    t = t.transpose(0, 2, 3, 1, 4).reshape(ng, NA_DR + 1, GRID_W, NA_GROUP * GRID_W)
    mb = meta_bias.astype(f32) * LOG2E
    mb = jnp.concatenate([mb, jnp.full((NA_HEADS, META_PAD - N_META), NEG, f32)], axis=1)
    tm = jnp.broadcast_to(mb.reshape(ng, NA_GROUP, META_PAD, 1), (ng, NA_GROUP, META_PAD, GRID_W))
    tm = tm.transpose(0, 2, 1, 3).reshape(ng, META_PAD, NA_GROUP * GRID_W)
    return t, tm


def _na_kernel(r0e_ref, blk_ref, q_ref, k_ref, vt_ref, km_ref, vtm_ref, t_ref, tm_ref, g_ref, wmask_ref,
               o_ref, s_ref, mx_ref):
    b = pl.program_id(1)
    gw = NA_GROUP * GRID_W
    lane_lo = lax.broadcasted_iota(jnp.int32, (gw, LANES), 1) < GRID_W
    head_even = (lax.broadcasted_iota(jnp.int32, (gw, LANES), 0) // NA_HEAD_DIM) % 2 == 0
    own_half = lane_lo == head_even
    last = NA_ROWS_PER_STEP - 1

    def scores(rl, slot):
        r = b * NA_ROWS_PER_STEP + rl
        q = q_ref[pl.ds(pl.multiple_of(rl * GRID_W, GRID_W), GRID_W), :]
        wt = jnp.concatenate([q] * NA_GROUP, axis=0) * wmask_ref[...]
        koff = pl.multiple_of(r0e_ref[r] * GRID_W, LANES)
        keys = jnp.concatenate([k_ref[pl.ds(koff, NA_SLAB), :], km_ref[...]], axis=0)
        bias = jnp.concatenate(
            [t_ref[0, blk_ref[r * NA_SLAB_ROWS + j]] for j in range(NA_SLAB_ROWS)] + [tm_ref[0]], axis=0)
        s = lax.dot_general(keys, wt, _NT, preferred_element_type=f32) + bias
        s_ref[slot] = s
        mx_ref[slot] = jnp.max(s, axis=0, keepdims=True)

    def finish(rl, slot):
        r = b * NA_ROWS_PER_STEP + rl
        koff = pl.multiple_of(r0e_ref[r] * GRID_W, LANES)
        p = jnp.exp2(s_ref[slot] - mx_ref[slot])
        l = jnp.sum(p, axis=0, keepdims=True)
        vals = jnp.concatenate([vt_ref[:, pl.ds(koff, NA_SLAB)], vtm_ref[...]], axis=1)
        out = jnp.dot(vals, p.astype(bf16), preferred_element_type=f32)
        rl_ = 1.0 / l
        c = jnp.concatenate([out[:LANES, :LANES] * rl_[:, :LANES], out[LANES:, LANES:] * rl_[:, LANES:]], axis=0)
        return jnp.where(own_half, c, pltpu.roll(c, GRID_W, axis=1))

    scores(0, 0)

    def pair(pi):
        ra = 2 * pi
        scores(ra + 1, 1)
        d0 = finish(ra, 0)
        scores(jnp.minimum(ra + 2, last), 0)
        d1 = finish(ra + 1, 1)
        y = jnp.where(lane_lo, d0, d1)
        loff = pl.multiple_of(pi * LANES, LANES)
        g = g_ref[:, pl.ds(loff, LANES)].astype(f32)
        o_ref[:, pl.ds(loff, LANES)] = (y * _silu(g)).astype(bf16)

    def trip(t, carry):
        for u in range(NA_PAIRS_PER_TRIP):
            pair(t * NA_PAIRS_PER_TRIP + u)
        return carry

    lax.fori_loop(0, NA_ROWS_PER_STEP // (2 * NA_PAIRS_PER_TRIP), trip, 0)


def _na(naq, nak, navt, nakm, navtm, t, tmeta, gnt):
    r0e, blk = _na_row_plan()
    wmask = jnp.asarray(np.kron(np.eye(NA_GROUP), np.ones((GRID_W, NA_HEAD_DIM))), bf16)
    ng = NA_HEADS // NA_GROUP
    gw = NA_GROUP * GRID_W
    tq = NA_ROWS_PER_STEP * GRID_W
    grid = (ng, ROWS // NA_ROWS_PER_STEP)
    grid_spec = pltpu.PrefetchScalarGridSpec(
        num_scalar_prefetch=2,
        grid=grid,
        in_specs=[
            pl.BlockSpec((tq, gw), lambda g, b, *_: (b, g)),
            pl.BlockSpec((SEQ, gw), lambda g, b, *_: (0, g)),
            pl.BlockSpec((gw, SEQ), lambda g, b, *_: (g, 0)),
            pl.BlockSpec((META_PAD, gw), lambda g, b, *_: (0, g)),
            pl.BlockSpec((gw, META_PAD), lambda g, b, *_: (g, 0)),
            pl.BlockSpec((1, NA_DR + 1, GRID_W, gw), lambda g, b, *_: (g, 0, 0, 0)),
            pl.BlockSpec((1, META_PAD, gw), lambda g, b, *_: (g, 0, 0)),
            pl.BlockSpec((gw, tq), lambda g, b, *_: (g, b)),
            pl.BlockSpec((gw, gw), lambda g, b, *_: (0, 0)),
        ],
        out_specs=pl.BlockSpec((gw, tq), lambda g, b, *_: (g, b)),
        scratch_shapes=[pltpu.VMEM((2, NA_KEYS, gw), f32), pltpu.VMEM((2, 1, gw), f32)],
    )
    return pl.pallas_call(
        _na_kernel,
        grid_spec=grid_spec,
        out_shape=jax.ShapeDtypeStruct((NA_WIDTH, SEQ), bf16),
        compiler_params=pltpu.CompilerParams(
            dimension_semantics=("arbitrary", "arbitrary"), vmem_limit_bytes=VMEM_LIMIT_BYTES),
        name="na",
    )(jnp.asarray(r0e), jnp.asarray(blk), naq, nak, navt, nakm, navtm, t, tmeta, gnt, wmask)


def _out_kernel(x_ref, ym_ref, yn_ref, wm_ref, wn_ref, o_ref):
    acc = lax.dot_general(ym_ref[...], wm_ref[...], _TN, preferred_element_type=f32)
    acc = acc + lax.dot_general(yn_ref[...], wn_ref[...], _TN, preferred_element_type=f32)
    o_ref[...] = x_ref[...] + acc


def _out_proj(x, ymt, ynt, wm, wn, tm):
    n = x.shape[0]
    return pl.pallas_call(
        _out_kernel,
        grid=(n // tm,),
        in_specs=[
            pl.BlockSpec((tm, D_MODEL), lambda i: (i, 0)),
            pl.BlockSpec((MLA_WIDTH, tm), lambda i: (0, i)),
            pl.BlockSpec((NA_WIDTH, tm), lambda i: (0, i)),
            _const_spec(wm.shape),
            _const_spec(wn.shape),
        ],
        out_specs=pl.BlockSpec((tm, D_MODEL), lambda i: (i, 0)),
        out_shape=jax.ShapeDtypeStruct((n, D_MODEL), f32),
        compiler_params=pltpu.CompilerParams(
            dimension_semantics=("arbitrary",), vmem_limit_bytes=VMEM_LIMIT_BYTES),
        name="out_proj",
    )(x, ymt, ynt, wm, wn)


def _prep_weights(norm_w, w_in, q_lat_norm_w, kv_lat_norm_w, w_uq, w_ukv,
                  mla_qn_w, mla_qpe_w, mla_kn_w, mla_kpe_w, na_q_norm_w, na_k_norm_w):
    sizes = (MLA_Q_RANK, MLA_KV_RANK, MLA_ROPE, MLA_WIDTH, NA_WIDTH, NA_WIDTH, NA_WIDTH, NA_WIDTH)
    o = np.concatenate([[0], np.cumsum(sizes)])
    seg = [w_in[:, o[i]:o[i + 1]] for i in range(8)]
    wa = jnp.concatenate([seg[0], seg[1], seg[4], seg[5]], axis=1).astype(bf16)
    wbt = jnp.concatenate([seg[3], seg[6], seg[7], seg[2]], axis=1).T.astype(bf16)
    ukv = w_ukv.reshape(MLA_KV_RANK, MLA_HEADS, 2, MLA_NOPE)
    return {
        "norm_w": norm_w.reshape(1, D_MODEL).astype(f32),
        "wa": wa,
        "wbt": wbt,
        "qlw": q_lat_norm_w.reshape(1, MLA_Q_RANK).astype(f32),
        "kvlw": kv_lat_norm_w.reshape(1, MLA_KV_RANK).astype(f32),
        "wuqt": w_uq.T.astype(bf16),
        "wuk": ukv[:, :, 0, :].reshape(MLA_KV_RANK, MLA_HEADS * MLA_NOPE).astype(bf16),
        "wuvt": ukv[:, :, 1, :].reshape(MLA_KV_RANK, MLA_HEADS * MLA_V).T.astype(bf16),
        "qnw": mla_qn_w.reshape(MLA_NOPE, 1).astype(f32),
        "qpew": mla_qpe_w.reshape(MLA_ROPE, 1).astype(f32),
        "knw": mla_kn_w.reshape(1, MLA_NOPE).astype(f32),
        "kpew": mla_kpe_w.reshape(MLA_ROPE, 1).astype(f32),
        "naqw": jnp.tile(na_q_norm_w.reshape(1, NA_HEAD_DIM), (1, 2)).astype(f32),
        "nakw": jnp.tile(na_k_norm_w.reshape(1, NA_HEAD_DIM), (1, 2)).astype(f32),
    }


def kernel(x, meta_tokens, norm_w, w_in, q_lat_norm_w, kv_lat_norm_w, w_uq, w_ukv,
           mla_qn_w, mla_qpe_w, mla_kn_w, mla_kpe_w, na_q_norm_w, na_k_norm_w,
           na_rel_bias, na_meta_bias, w_out):
    assert x.shape == (1, SEQ, D_MODEL) and norm_w.shape[0] == 1
    xr = x[0]
    w = _prep_weights(norm_w[0], w_in[0], q_lat_norm_w[0], kv_lat_norm_w[0], w_uq[0], w_ukv[0],
                      mla_qn_w[0], mla_qpe_w[0], mla_kn_w[0], mla_kpe_w[0],
                      na_q_norm_w[0], na_k_norm_w[0])

    pos = jnp.arange(N_META + SEQ, dtype=f32)
    inv_freq = ROPE_BASE ** (-(jnp.arange(0, MLA_ROPE, 2, dtype=f32) / MLA_ROPE))
    ang = pos[:, None] * inv_freq[None, :]
    cos_t, sin_t = jnp.cos(ang).T, jnp.sin(ang).T
    pad = ((0, 0), (0, META_PAD - N_META))
    cos_m, sin_m = jnp.pad(cos_t[:, :N_META], pad), jnp.pad(sin_t[:, :N_META], pad)

    xm = jnp.pad(meta_tokens.astype(f32), ((0, META_PAD - N_META), (0, 0)))

    qt, k, vt, gmt, naq, nak, navt, gnt = _project(xr, cos_t[:, N_META:], sin_t[:, N_META:], w, tm=256)
    _, km, vtm, _, _, nakm, navtm, _ = _project(xm, cos_m, sin_m, w, tm=META_PAD)

    ymt = _mla(qt, k, vt, km, vtm, gmt, tq=1024, tk=1024)

    t, tmeta = _na_tables(na_rel_bias[0], na_meta_bias[0])
    ynt = _na(naq, nak, navt, nakm, navtm, t, tmeta, gnt)

    wo = w_out[0].astype(bf16)
    out = _out_proj(xr, ymt, ynt, wo[:MLA_WIDTH], wo[MLA_WIDTH:], tm=512)
    return out[None]
```

```python
import functools
import math

import numpy as np
import jax
import jax.numpy as jnp
from jax import lax
from jax.experimental import pallas as pl
from jax.experimental.pallas import tpu as pltpu

D_MODEL = 1024
SEQ = 16384
N_META = 16
GRID_W = 64
ROWS = SEQ // GRID_W
EPS = 1e-6

MLA_HEADS = 8
MLA_NOPE = 128
MLA_ROPE = 64
MLA_V = 128
MLA_Q_RANK = 256
MLA_KV_RANK = 256
MLA_WIDTH = MLA_HEADS * MLA_V
ROPE_BASE = 10000.0

NA_HEADS = 16
NA_HEAD_DIM = 64
NA_WIDTH = NA_HEADS * NA_HEAD_DIM
NA_WIN_R = 8
NA_WIN_C = 16

LOG2E = math.log2(math.e)
NEG = -1e30

LANES = 128
MXU_DIM = 256
VMEM_LIMIT_BYTES = 56 * 1024 * 1024

META_PAD = 128
QK_DIM = MXU_DIM
V_ROWS = MLA_V + 16

NA_GROUP = 4
NA_SLAB_ROWS = 10
NA_SLAB = NA_SLAB_ROWS * GRID_W
NA_ROWS_PER_STEP = 32
NA_PAIRS_PER_TRIP = 8
NA_KEYS = NA_SLAB + META_PAD

_NT = (((1,), (1,)), ((), ()))
_TN = (((0,), (0,)), ((), ()))

f32 = jnp.float32
bf16 = jnp.bfloat16

MLA_QK_FP8 = jnp.float8_e4m3fn
MLA_QK_SCALE = math.sqrt((MLA_NOPE + MLA_ROPE) ** -0.5 * LOG2E)
MLA_FIXED_OFFSET_MAX = 60.0
MLA_TILES_PER_TRIP = 8


def _rms(x, axis):
    return lax.rsqrt(jnp.mean(x * x, axis=axis, keepdims=True) + EPS)


def _silu(g):
    return g / (1.0 + jnp.exp(-g))


def _proj_kernel(x_ref, cos_ref, sin_ref, nw_ref, wa_ref, wbt_ref, qlw_ref, kvlw_ref,
                 wuqt_ref, wuk_ref, wuvt_ref, qnw_ref, qpew_ref, knw_ref, kpew_ref,
                 naqw_ref, nakw_ref,
                 qt_ref, k_ref, qt8_ref, k8_ref, vt_ref, gmt_ref, naq_ref, nak_ref, navt_ref, gnt_ref):
    tm = x_ref.shape[0]
    x = x_ref[...]
    h = (x * _rms(x, 1) * nw_ref[...]).astype(bf16)
    pa = jnp.dot(h, wa_ref[...], preferred_element_type=f32)
    pbt = lax.dot_general(wbt_ref[...], h, _NT, preferred_element_type=f32)

    gmt_ref[...] = pbt[0:MLA_WIDTH].astype(bf16)
    navt_ref[...] = pbt[MLA_WIDTH:MLA_WIDTH + NA_WIDTH].astype(bf16)
    gnt_ref[...] = pbt[MLA_WIDTH + NA_WIDTH:MLA_WIDTH + 2 * NA_WIDTH].astype(bf16)

    cos = cos_ref[...]
    sin = sin_ref[...]
    half = MLA_ROPE // 2

    kpe = pbt[MLA_WIDTH + 2 * NA_WIDTH:MLA_WIDTH + 2 * NA_WIDTH + MLA_ROPE]
    kpe = kpe * _rms(kpe, 0) * kpew_ref[...]
    k1, k2 = kpe[:half], kpe[half:]
    kro = jnp.concatenate(
        [k1 * cos - k2 * sin, k2 * cos + k1 * sin, jnp.zeros((LANES - MLA_ROPE, tm), f32)], axis=0)
    kpe_nat = kro.T * MLA_QK_SCALE

    qscale = MLA_QK_SCALE
    ql = pa[:, 0:MLA_Q_RANK]
    qln = (ql * _rms(ql, 1) * qlw_ref[...]).astype(bf16)
    qt = lax.dot_general(wuqt_ref[...], qln, _NT, preferred_element_type=f32)
    per_head = MLA_NOPE + MLA_ROPE
    for hh in range(MLA_HEADS):
        base = per_head * hh
        qn = qt[base:base + MLA_NOPE]
        qn = qn * _rms(qn, 0) * qnw_ref[...]
        qp = qt[base + MLA_NOPE:base + per_head]
        qp = qp * _rms(qp, 0) * qpew_ref[...]
        q1, q2 = qp[:half], qp[half:]
        qh = jnp.concatenate([qn, q1 * cos - q2 * sin, q2 * cos + q1 * sin], axis=0) * qscale
        for dst in (qt_ref, qt8_ref):
            dst[hh, 0:per_head, :] = qh.astype(dst.dtype)
            dst[hh, per_head:QK_DIM, :] = jnp.zeros((QK_DIM - per_head, tm), dst.dtype)

    kvl = pa[:, MLA_Q_RANK:MLA_Q_RANK + MLA_KV_RANK]
    kvn = (kvl * _rms(kvl, 1) * kvlw_ref[...]).astype(bf16)
    kn = jnp.dot(kvn, wuk_ref[...], preferred_element_type=f32)
    vt = lax.dot_general(wuvt_ref[...], kvn, _NT, preferred_element_type=f32)
    for hh in range(MLA_HEADS):
        xh = kn[:, MLA_NOPE * hh:MLA_NOPE * (hh + 1)]
        xh = xh * _rms(xh, 1) * knw_ref[...]
        for dst in (k_ref, k8_ref):
            dst[hh, :, 0:MLA_NOPE] = (xh * MLA_QK_SCALE).astype(dst.dtype)
            dst[hh, :, MLA_NOPE:QK_DIM] = kpe_nat.astype(dst.dtype)
        vt_ref[hh, 0:MLA_V, :] = vt[MLA_V * hh:MLA_V * (hh + 1)].astype(bf16)
        vt_ref[hh, MLA_V:V_ROWS, :] = jnp.ones((V_ROWS - MLA_V, tm), bf16)

    lane = lax.broadcasted_iota(jnp.int32, (tm, LANES), 1)
    lo = lane < NA_HEAD_DIM
    na_scale = NA_HEAD_DIM ** -0.5 * LOG2E
    q_off = MLA_Q_RANK + MLA_KV_RANK
    k_off = q_off + NA_WIDTH
    for p in range(NA_WIDTH // LANES):
        for off, w_ref, dst, scale in ((q_off, naqw_ref, naq_ref, na_scale), (k_off, nakw_ref, nak_ref, 1.0)):
            xp = pa[:, off + LANES * p:off + LANES * (p + 1)]
            sq = xp * xp
            s_lo = jnp.sum(jnp.where(lo, sq, 0.0), axis=1, keepdims=True)
            s_hi = jnp.sum(jnp.where(lo, 0.0, sq), axis=1, keepdims=True)
            r = jnp.where(lo, lax.rsqrt(s_lo / NA_HEAD_DIM + EPS), lax.rsqrt(s_hi / NA_HEAD_DIM + EPS))
            y = xp * r * w_ref[...]
            if scale != 1.0:
                y = y * scale
            dst[:, LANES * p:LANES * (p + 1)] = y.astype(bf16)


def _const_spec(shape):
    nd = len(shape)
    return pl.BlockSpec(shape, lambda i, _nd=nd: (0,) * _nd, pipeline_mode=pl.Buffered(1))


def _project(x, cos_t, sin_t, w, tm):
    n = x.shape[0]
    grid = (n // tm,)
    weights = (w["norm_w"], w["wa"], w["wbt"], w["qlw"], w["kvlw"], w["wuqt"], w["wuk"], w["wuvt"],
               w["qnw"], w["qpew"], w["knw"], w["kpew"], w["naqw"], w["nakw"])
    in_specs = [pl.BlockSpec((tm, D_MODEL), lambda i: (i, 0)),
                pl.BlockSpec((MLA_ROPE // 2, tm), lambda i: (0, i)),
                pl.BlockSpec((MLA_ROPE // 2, tm), lambda i: (0, i))]
    in_specs += [_const_spec(a.shape) for a in weights]
    out_shape = (
        jax.ShapeDtypeStruct((MLA_HEADS, QK_DIM, n), bf16),
        jax.ShapeDtypeStruct((MLA_HEADS, n, QK_DIM), bf16),
        jax.ShapeDtypeStruct((MLA_HEADS, QK_DIM, n), MLA_QK_FP8),
        jax.ShapeDtypeStruct((MLA_HEADS, n, QK_DIM), MLA_QK_FP8),
        jax.ShapeDtypeStruct((MLA_HEADS, V_ROWS, n), bf16),
        jax.ShapeDtypeStruct((MLA_WIDTH, n), bf16),
        jax.ShapeDtypeStruct((n, NA_WIDTH), bf16),
        jax.ShapeDtypeStruct((n, NA_WIDTH), bf16),
        jax.ShapeDtypeStruct((NA_WIDTH, n), bf16),
        jax.ShapeDtypeStruct((NA_WIDTH, n), bf16),
    )
    out_specs = (
        pl.BlockSpec((MLA_HEADS, QK_DIM, tm), lambda i: (0, 0, i)),
        pl.BlockSpec((MLA_HEADS, tm, QK_DIM), lambda i: (0, i, 0)),
        pl.BlockSpec((MLA_HEADS, QK_DIM, tm), lambda i: (0, 0, i)),
        pl.BlockSpec((MLA_HEADS, tm, QK_DIM), lambda i: (0, i, 0)),
        pl.BlockSpec((MLA_HEADS, V_ROWS, tm), lambda i: (0, 0, i)),
        pl.BlockSpec((MLA_WIDTH, tm), lambda i: (0, i)),
        pl.BlockSpec((tm, NA_WIDTH), lambda i: (i, 0)),
        pl.BlockSpec((tm, NA_WIDTH), lambda i: (i, 0)),
        pl.BlockSpec((NA_WIDTH, tm), lambda i: (0, i)),
        pl.BlockSpec((NA_WIDTH, tm), lambda i: (0, i)),
    )
    return pl.pallas_call(
        _proj_kernel,
        grid=grid,
        in_specs=in_specs,
        out_specs=out_specs,
        out_shape=out_shape,
        compiler_params=pltpu.CompilerParams(
            dimension_semantics=("arbitrary",), vmem_limit_bytes=VMEM_LIMIT_BYTES),
        name="proj",
    )(x, cos_t, sin_t, *weights)


def _mla_online(qt, k_ref, vt_ref, km_ref, vtm_ref, acc_ref, m_ref, mx_ref, s_ref, tk):
    nk = k_ref.shape[1] // tk
    assert nk % 2 == 0 and nk >= 2

    def scores(i, slot):
        off = pl.multiple_of(i * tk, tk)
        s = jnp.dot(k_ref[0, pl.ds(off, tk), :], qt, preferred_element_type=f32)
        s_ref[slot] = s
        mx_ref[slot] = jnp.max(s, axis=0, keepdims=True)

    def accumulate(i, slot):
        off = pl.multiple_of(i * tk, tk)
        m_prev = m_ref[...]
        m_new = jnp.maximum(m_prev, mx_ref[slot])
        alpha = jnp.exp2(m_prev - m_new)
        p = jnp.exp2(s_ref[slot] - m_new).astype(bf16)
        pv = jnp.dot(vt_ref[0, :, pl.ds(off, tk)], p, preferred_element_type=f32)
        acc_ref[...] = alpha * acc_ref[...] + pv
        m_ref[...] = m_new

    scores(0, 0)

    s = jnp.dot(km_ref[0], qt, preferred_element_type=f32)
    row = lax.broadcasted_iota(jnp.int32, s.shape, 0)
    s = jnp.where(row < N_META, s, NEG)
    m0 = jnp.max(s, axis=0, keepdims=True)
    p = jnp.exp2(s - m0).astype(bf16)
    acc_ref[...] = jnp.dot(vtm_ref[0], p, preferred_element_type=f32)
    m_ref[...] = m0

    def body(j, carry):
        scores(2 * j + 1, 1)
        accumulate(2 * j, 0)
        scores(2 * j + 2, 0)
        accumulate(2 * j + 1, 1)
        return carry

    lax.fori_loop(0, nk // 2 - 1, body, 0)
    scores(nk - 1, 1)
    accumulate(nk - 2, 0)
    accumulate(nk - 1, 1)


def _mla_fixed_offset(c, qt, k_ref, vt_ref, km_ref, vtm_ref, acc_ref, tk):
    nk = k_ref.shape[1] // tk
    assert nk % MLA_TILES_PER_TRIP == 0

    s = jnp.dot(km_ref[0], qt, preferred_element_type=f32)
    row = lax.broadcasted_iota(jnp.int32, s.shape, 0)
    p = jnp.exp2(jnp.where(row < N_META, s - c, NEG)).astype(bf16)
    acc_ref[...] = jnp.dot(vtm_ref[0], p, preferred_element_type=f32)

    def tile(i):
        off = pl.multiple_of(i * tk, tk)
        for qb in range(qt.shape[1] // MXU_DIM):
            cols = slice(qb * MXU_DIM, (qb + 1) * MXU_DIM)
            s = jnp.dot(k_ref[0, pl.ds(off, tk), :], qt[:, cols], preferred_element_type=f32)
            p = jnp.exp2(s - c).astype(bf16)
            acc_ref[:, cols] += jnp.dot(vt_ref[0, :, pl.ds(off, tk)], p, preferred_element_type=f32)

    def body(j, carry):
        for u in range(MLA_TILES_PER_TRIP):
            tile(j * MLA_TILES_PER_TRIP + u)
        return carry

    lax.fori_loop(0, nk // MLA_TILES_PER_TRIP, body, 0)


def _mla_kernel(c_ref, qt_ref, k_ref, km_ref, qt8_ref, k8_ref, km8_ref, vt_ref, vtm_ref, g_ref, o_ref,
                acc_ref, m_ref, mx_ref, s_ref, *, tk):
    c = c_ref[0]
    bounded = c <= MLA_FIXED_OFFSET_MAX

    @pl.when(bounded)
    def _():
        _mla_fixed_offset(c, qt8_ref[0], k8_ref, vt_ref, km8_ref, vtm_ref, acc_ref, tk)

    @pl.when(jnp.logical_not(bounded))
    def _():
        _mla_online(qt_ref[0], k_ref, vt_ref, km_ref, vtm_ref, acc_ref, m_ref, mx_ref, s_ref, tk)

    acc = acc_ref[...]
    o = acc[0:MLA_V] / acc[MLA_V:MLA_V + 1]
    g = g_ref[...].astype(f32)
    o_ref[...] = (o * _silu(g)).astype(bf16)


def _mla_score_bound(qn_w, qpe_w, kn_w, kpe_w):
    def sq(nope_w, pe_w):
        return MLA_NOPE * jnp.max(nope_w.astype(f32) ** 2) + MLA_ROPE * jnp.max(pe_w.astype(f32) ** 2)
    margin = (1.0 + 2.0 ** -4) ** 2 * 1.01
    return (margin * MLA_QK_SCALE ** 2 * jnp.sqrt(sq(qn_w, qpe_w) * sq(kn_w, kpe_w))).reshape(1)


def _mla(c, qt, k, km, qt8, k8, km8, vt, vtm, gmt, tq, tk):
    n = qt.shape[2]
    grid = (MLA_HEADS, n // tq)
    return pl.pallas_call(
        functools.partial(_mla_kernel, tk=tk),
        grid=grid,
        in_specs=[
            pl.BlockSpec(memory_space=pltpu.SMEM),
            pl.BlockSpec((1, QK_DIM, tq), lambda h, j: (h, 0, j)),
            pl.BlockSpec((1, n, QK_DIM), lambda h, j: (h, 0, 0)),
            pl.BlockSpec((1, META_PAD, QK_DIM), lambda h, j: (h, 0, 0)),
            pl.BlockSpec((1, QK_DIM, tq), lambda h, j: (h, 0, j)),
            pl.BlockSpec((1, n, QK_DIM), lambda h, j: (h, 0, 0)),
            pl.BlockSpec((1, META_PAD, QK_DIM), lambda h, j: (h, 0, 0)),
            pl.BlockSpec((1, V_ROWS, n), lambda h, j: (h, 0, 0)),
            pl.BlockSpec((1, V_ROWS, META_PAD), lambda h, j: (h, 0, 0)),
            pl.BlockSpec((MLA_V, tq), lambda h, j: (h, j)),
        ],
        out_specs=pl.BlockSpec((MLA_V, tq), lambda h, j: (h, j)),
        out_shape=jax.ShapeDtypeStruct((MLA_WIDTH, n), bf16),
        scratch_shapes=[pltpu.VMEM((V_ROWS, tq), f32), pltpu.VMEM((1, tq), f32),
                        pltpu.VMEM((2, 1, tq), f32), pltpu.VMEM((2, tk, tq), f32)],
        compiler_params=pltpu.CompilerParams(
            dimension_semantics=("arbitrary", "arbitrary"), vmem_limit_bytes=VMEM_LIMIT_BYTES),
        name="mla",
    )(c, qt, k, km, qt8, k8, km8, vt, vtm, gmt)


NA_DR = 2 * NA_WIN_R - 1
NA_DC = 2 * NA_WIN_C - 1


def _na_row_plan():
    r = np.arange(ROWS)
    r0 = np.clip(r - NA_WIN_R // 2, 0, ROWS - NA_WIN_R)
    r0e = np.minimum(r0 - (r0 % 2), ROWS - NA_SLAB_ROWS)
    kr = r0e[:, None] + np.arange(NA_SLAB_ROWS)[None, :]
    in_win = (kr >= r0[:, None]) & (kr < r0[:, None] + NA_WIN_R)
    dr = kr - r[:, None] + (NA_WIN_R - 1)
    blk = np.where(in_win, dr, NA_DR)
    assert blk.min() >= 0 and blk.max() <= NA_DR
    return r0e.astype(np.int32), blk.reshape(-1).astype(np.int32)


def _na_tables(rel_bias, meta_bias):
    rel = rel_bias.astype(f32)
    half = NA_WIN_C
    w = jnp.concatenate([rel[..., half - 1::-1], jnp.zeros(rel.shape[:2] + (LANES - NA_DC,), f32),
                         rel[..., :half - 1:-1]], axis=-1)
    flat = jnp.tile(w, (1, 1, GRID_W))[..., :GRID_W * (LANES - 1)]
    toe = flat.reshape(NA_HEADS, NA_DR, GRID_W, LANES - 1)[..., :GRID_W]
    c = np.arange(GRID_W)
    c0 = np.clip(c - NA_WIN_C // 2, 0, GRID_W - NA_WIN_C)
    kc = np.arange(GRID_W)
    valid_c = (kc[:, None] >= c0[None, :]) & (kc[:, None] < c0[None, :] + NA_WIN_C)
    t = jnp.where(valid_c[None, None], toe * LOG2E, NEG)
    t = jnp.concatenate([t, jnp.full((NA_HEADS, 1, GRID_W, GRID_W), NEG, f32)], axis=1)
    ng = NA_HEADS // NA_GROUP
    t = t.reshape(ng, NA_GROUP, NA_DR + 1, GRID_W, GRID_W)
    t = t.transpose(0, 2, 3, 1, 4).reshape(ng, NA_DR + 1, GRID_W, NA_GROUP * GRID_W)
    mb = meta_bias.astype(f32) * LOG2E
    mb = jnp.concatenate([mb, jnp.full((NA_HEADS, META_PAD - N_META), NEG, f32)], axis=1)
    tm = jnp.broadcast_to(mb.reshape(ng, NA_GROUP, META_PAD, 1), (ng, NA_GROUP, META_PAD, GRID_W))
    tm = tm.transpose(0, 2, 1, 3).reshape(ng, META_PAD, NA_GROUP * GRID_W)
    return t, tm


def _na_kernel(r0e_ref, blk_ref, q_ref, k_ref, vt_ref, km_ref, vtm_ref, t_ref, tm_ref, g_ref, wmask_ref,
               o_ref, s_ref, mx_ref):
    b = pl.program_id(1)
    gw = NA_GROUP * GRID_W
    lane_lo = lax.broadcasted_iota(jnp.int32, (gw, LANES), 1) < GRID_W
    head_even = (lax.broadcasted_iota(jnp.int32, (gw, LANES), 0) // NA_HEAD_DIM) % 2 == 0
    own_half = lane_lo == head_even
    last = NA_ROWS_PER_STEP - 1

    def scores(rl, slot):
        r = b * NA_ROWS_PER_STEP + rl
        q = q_ref[pl.ds(pl.multiple_of(rl * GRID_W, GRID_W), GRID_W), :]
        wt = jnp.concatenate([q] * NA_GROUP, axis=0) * wmask_ref[...]
        koff = pl.multiple_of(r0e_ref[r] * GRID_W, LANES)
        keys = jnp.concatenate([k_ref[pl.ds(koff, NA_SLAB), :], km_ref[...]], axis=0)
        bias = jnp.concatenate(
            [t_ref[0, blk_ref[r * NA_SLAB_ROWS + j]] for j in range(NA_SLAB_ROWS)] + [tm_ref[0]], axis=0)
        s = lax.dot_general(keys, wt, _NT, preferred_element_type=f32) + bias
        s_ref[slot] = s
        mx_ref[slot] = jnp.max(s, axis=0, keepdims=True)

    def finish(rl, slot):
        r = b * NA_ROWS_PER_STEP + rl
        koff = pl.multiple_of(r0e_ref[r] * GRID_W, LANES)
        p = jnp.exp2(s_ref[slot] - mx_ref[slot])
        l = jnp.sum(p, axis=0, keepdims=True)
        vals = jnp.concatenate([vt_ref[:, pl.ds(koff, NA_SLAB)], vtm_ref[...]], axis=1)
        out = jnp.dot(vals, p.astype(bf16), preferred_element_type=f32)
        rl_ = 1.0 / l
        c = jnp.concatenate([out[:LANES, :LANES] * rl_[:, :LANES], out[LANES:, LANES:] * rl_[:, LANES:]], axis=0)
        return jnp.where(own_half, c, pltpu.roll(c, GRID_W, axis=1))

    scores(0, 0)

    def pair(pi):
        ra = 2 * pi
        scores(ra + 1, 1)
        d0 = finish(ra, 0)
        scores(jnp.minimum(ra + 2, last), 0)
        d1 = finish(ra + 1, 1)
        y = jnp.where(lane_lo, d0, d1)
        loff = pl.multiple_of(pi * LANES, LANES)
        g = g_ref[:, pl.ds(loff, LANES)].astype(f32)
        o_ref[:, pl.ds(loff, LANES)] = (y * _silu(g)).astype(bf16)

    def trip(t, carry):
        for u in range(NA_PAIRS_PER_TRIP):
            pair(t * NA_PAIRS_PER_TRIP + u)
        return carry

    lax.fori_loop(0, NA_ROWS_PER_STEP // (2 * NA_PAIRS_PER_TRIP), trip, 0)


def _na(naq, nak, navt, nakm, navtm, t, tmeta, gnt):
    r0e, blk = _na_row_plan()
    wmask = jnp.asarray(np.kron(np.eye(NA_GROUP), np.ones((GRID_W, NA_HEAD_DIM))), bf16)
    ng = NA_HEADS // NA_GROUP
    gw = NA_GROUP * GRID_W
    tq = NA_ROWS_PER_STEP * GRID_W
    grid = (ng, ROWS // NA_ROWS_PER_STEP)
    grid_spec = pltpu.PrefetchScalarGridSpec(
        num_scalar_prefetch=2,
        grid=grid,
        in_specs=[
            pl.BlockSpec((tq, gw), lambda g, b, *_: (b, g)),
            pl.BlockSpec((SEQ, gw), lambda g, b, *_: (0, g)),
            pl.BlockSpec((gw, SEQ), lambda g, b, *_: (g, 0)),
            pl.BlockSpec((META_PAD, gw), lambda g, b, *_: (0, g)),
            pl.BlockSpec((gw, META_PAD), lambda g, b, *_: (g, 0)),
            pl.BlockSpec((1, NA_DR + 1, GRID_W, gw), lambda g, b, *_: (g, 0, 0, 0)),
            pl.BlockSpec((1, META_PAD, gw), lambda g, b, *_: (g, 0, 0)),
            pl.BlockSpec((gw, tq), lambda g, b, *_: (g, b)),
            pl.BlockSpec((gw, gw), lambda g, b, *_: (0, 0)),
        ],
        out_specs=pl.BlockSpec((gw, tq), lambda g, b, *_: (g, b)),
        scratch_shapes=[pltpu.VMEM((2, NA_KEYS, gw), f32), pltpu.VMEM((2, 1, gw), f32)],
    )
    return pl.pallas_call(
        _na_kernel,
        grid_spec=grid_spec,
        out_shape=jax.ShapeDtypeStruct((NA_WIDTH, SEQ), bf16),
        compiler_params=pltpu.CompilerParams(
            dimension_semantics=("arbitrary", "arbitrary"), vmem_limit_bytes=VMEM_LIMIT_BYTES),
        name="na",
    )(jnp.asarray(r0e), jnp.asarray(blk), naq, nak, navt, nakm, navtm, t, tmeta, gnt, wmask)


def _out_kernel(x_ref, ym_ref, yn_ref, wm_ref, wn_ref, o_ref):
    acc = lax.dot_general(ym_ref[...], wm_ref[...], _TN, preferred_element_type=f32)
    acc = acc + lax.dot_general(yn_ref[...], wn_ref[...], _TN, preferred_element_type=f32)
    o_ref[...] = x_ref[...] + acc


def _out_proj(x, ymt, ynt, wm, wn, tm):
    n = x.shape[0]
    return pl.pallas_call(
        _out_kernel,
        grid=(n // tm,),
        in_specs=[
            pl.BlockSpec((tm, D_MODEL), lambda i: (i, 0)),
            pl.BlockSpec((MLA_WIDTH, tm), lambda i: (0, i)),
            pl.BlockSpec((NA_WIDTH, tm), lambda i: (0, i)),
            _const_spec(wm.shape),
            _const_spec(wn.shape),
        ],
        out_specs=pl.BlockSpec((tm, D_MODEL), lambda i: (i, 0)),
        out_shape=jax.ShapeDtypeStruct((n, D_MODEL), f32),
        compiler_params=pltpu.CompilerParams(
            dimension_semantics=("arbitrary",), vmem_limit_bytes=VMEM_LIMIT_BYTES),
        name="out_proj",
    )(x, ymt, ynt, wm, wn)


def _prep_weights(norm_w, w_in, q_lat_norm_w, kv_lat_norm_w, w_uq, w_ukv,
                  mla_qn_w, mla_qpe_w, mla_kn_w, mla_kpe_w, na_q_norm_w, na_k_norm_w):
    sizes = (MLA_Q_RANK, MLA_KV_RANK, MLA_ROPE, MLA_WIDTH, NA_WIDTH, NA_WIDTH, NA_WIDTH, NA_WIDTH)
    o = np.concatenate([[0], np.cumsum(sizes)])
    seg = [w_in[:, o[i]:o[i + 1]] for i in range(8)]
    wa = jnp.concatenate([seg[0], seg[1], seg[4], seg[5]], axis=1).astype(bf16)
    wbt = jnp.concatenate([seg[3], seg[6], seg[7], seg[2]], axis=1).T.astype(bf16)
    ukv = w_ukv.reshape(MLA_KV_RANK, MLA_HEADS, 2, MLA_NOPE)
    return {
        "norm_w": norm_w.reshape(1, D_MODEL).astype(f32),
        "wa": wa,
        "wbt": wbt,
        "qlw": q_lat_norm_w.reshape(1, MLA_Q_RANK).astype(f32),
        "kvlw": kv_lat_norm_w.reshape(1, MLA_KV_RANK).astype(f32),
        "wuqt": w_uq.T.astype(bf16),
        "wuk": ukv[:, :, 0, :].reshape(MLA_KV_RANK, MLA_HEADS * MLA_NOPE).astype(bf16),
        "wuvt": ukv[:, :, 1, :].reshape(MLA_KV_RANK, MLA_HEADS * MLA_V).T.astype(bf16),
        "qnw": mla_qn_w.reshape(MLA_NOPE, 1).astype(f32),
        "qpew": mla_qpe_w.reshape(MLA_ROPE, 1).astype(f32),
        "knw": mla_kn_w.reshape(1, MLA_NOPE).astype(f32),
        "kpew": mla_kpe_w.reshape(MLA_ROPE, 1).astype(f32),
        "naqw": jnp.tile(na_q_norm_w.reshape(1, NA_HEAD_DIM), (1, 2)).astype(f32),
        "nakw": jnp.tile(na_k_norm_w.reshape(1, NA_HEAD_DIM), (1, 2)).astype(f32),
    }


def kernel(x, meta_tokens, norm_w, w_in, q_lat_norm_w, kv_lat_norm_w, w_uq, w_ukv,
           mla_qn_w, mla_qpe_w, mla_kn_w, mla_kpe_w, na_q_norm_w, na_k_norm_w,
           na_rel_bias, na_meta_bias, w_out):
    assert x.shape == (1, SEQ, D_MODEL) and norm_w.shape[0] == 1
    xr = x[0]
    w = _prep_weights(norm_w[0], w_in[0], q_lat_norm_w[0], kv_lat_norm_w[0], w_uq[0], w_ukv[0],
                      mla_qn_w[0], mla_qpe_w[0], mla_kn_w[0], mla_kpe_w[0],
                      na_q_norm_w[0], na_k_norm_w[0])

    pos = jnp.arange(N_META + SEQ, dtype=f32)
    inv_freq = ROPE_BASE ** (-(jnp.arange(0, MLA_ROPE, 2, dtype=f32) / MLA_ROPE))
    ang = pos[:, None] * inv_freq[None, :]
    cos_t, sin_t = jnp.cos(ang).T, jnp.sin(ang).T
    pad = ((0, 0), (0, META_PAD - N_META))
    cos_m, sin_m = jnp.pad(cos_t[:, :N_META], pad), jnp.pad(sin_t[:, :N_META], pad)

    xm = jnp.pad(meta_tokens.astype(f32), ((0, META_PAD - N_META), (0, 0)))

    qt, k, qt8, k8, vt, gmt, naq, nak, navt, gnt = _project(xr, cos_t[:, N_META:], sin_t[:, N_META:], w, tm=256)
    _, km, _, km8, vtm, _, _, nakm, navtm, _ = _project(xm, cos_m, sin_m, w, tm=META_PAD)

    c = _mla_score_bound(mla_qn_w[0], mla_qpe_w[0], mla_kn_w[0], mla_kpe_w[0])
    ymt = _mla(c, qt, k, km, qt8, k8, km8, vt, vtm, gmt, tq=1024, tk=1024)

    t, tmeta = _na_tables(na_rel_bias[0], na_meta_bias[0])
    ynt = _na(naq, nak, navt, nakm, navtm, t, tmeta, gnt)

    wo = w_out[0].astype(bf16)
    out = _out_proj(xr, ymt, ynt, wo[:MLA_WIDTH], wo[MLA_WIDTH:], tm=512)
    return out[None]
```

```python
import functools
import math

import numpy as np
import jax
import jax.numpy as jnp
from jax import lax
from jax.experimental import pallas as pl
from jax.experimental.pallas import tpu as pltpu

D_MODEL = 1024
SEQ = 16384
N_META = 16
GRID_W = 64
ROWS = SEQ // GRID_W
EPS = 1e-6

MLA_HEADS = 8
MLA_NOPE = 128
MLA_ROPE = 64
MLA_V = 128
MLA_Q_RANK = 256
MLA_KV_RANK = 256
MLA_WIDTH = MLA_HEADS * MLA_V
ROPE_BASE = 10000.0

NA_HEADS = 16
NA_HEAD_DIM = 64
NA_WIDTH = NA_HEADS * NA_HEAD_DIM
NA_WIN_R = 8
NA_WIN_C = 16

LOG2E = math.log2(math.e)
NEG = -1e30

LANES = 128
MXU_DIM = 256
VMEM_LIMIT_BYTES = 56 * 1024 * 1024

META_PAD = 128
QK_DIM = MXU_DIM
V_ROWS = MLA_V + 16

NA_GROUP = 4
NA_SLAB_ROWS = 10
NA_SLAB = NA_SLAB_ROWS * GRID_W
NA_ROWS_PER_STEP = 32
NA_PAIRS_PER_TRIP = 8
NA_KEYS = NA_SLAB + META_PAD

_NT = (((1,), (1,)), ((), ()))
_TN = (((0,), (0,)), ((), ()))

f32 = jnp.float32
bf16 = jnp.bfloat16

MLA_QK_FP8 = jnp.float8_e4m3fn
MLA_QK_SCALE = math.sqrt((MLA_NOPE + MLA_ROPE) ** -0.5 * LOG2E)
MLA_FIXED_OFFSET_MAX = 60.0
MLA_TILES_PER_TRIP = 8


def _rms(x, axis):
    return lax.rsqrt(jnp.mean(x * x, axis=axis, keepdims=True) + EPS)


def _silu(g):
    return g / (1.0 + jnp.exp(-g))


def _proj_kernel(x_ref, cos_ref, sin_ref, nw_ref, wa_ref, wbt_ref, qlw_ref, kvlw_ref,
                 wuqt_ref, wuk_ref, wuvt_ref, qnw_ref, qpew_ref, knw_ref, kpew_ref,
                 naqw_ref, nakw_ref,
                 qt_ref, k_ref, qt8_ref, k8_ref, vt_ref, gmt_ref, naq_ref, nak_ref, navt_ref, gnt_ref):
    tm = x_ref.shape[0]
    x = x_ref[...]
    h = (x * _rms(x, 1) * nw_ref[...]).astype(bf16)
    pa = jnp.dot(h, wa_ref[...], preferred_element_type=f32)
    pbt = lax.dot_general(wbt_ref[...], h, _NT, preferred_element_type=f32)

    gmt_ref[...] = pbt[0:MLA_WIDTH].astype(bf16)
    navt_ref[...] = pbt[MLA_WIDTH:MLA_WIDTH + NA_WIDTH].astype(bf16)
    gnt_ref[...] = pbt[MLA_WIDTH + NA_WIDTH:MLA_WIDTH + 2 * NA_WIDTH].astype(bf16)

    cos = cos_ref[...]
    sin = sin_ref[...]
    half = MLA_ROPE // 2

    kpe = pbt[MLA_WIDTH + 2 * NA_WIDTH:MLA_WIDTH + 2 * NA_WIDTH + MLA_ROPE]
    kpe = kpe * _rms(kpe, 0) * kpew_ref[...]
    k1, k2 = kpe[:half], kpe[half:]
    kro = jnp.concatenate(
        [k1 * cos - k2 * sin, k2 * cos + k1 * sin, jnp.zeros((LANES - MLA_ROPE, tm), f32)], axis=0)
    kpe_nat = kro.T * MLA_QK_SCALE

    qscale = MLA_QK_SCALE
    ql = pa[:, 0:MLA_Q_RANK]
    qln = (ql * _rms(ql, 1) * qlw_ref[...]).astype(bf16)
    qt = lax.dot_general(wuqt_ref[...], qln, _NT, preferred_element_type=f32)
    per_head = MLA_NOPE + MLA_ROPE
    for hh in range(MLA_HEADS):
        base = per_head * hh
        qn = qt[base:base + MLA_NOPE]
        qn = qn * _rms(qn, 0) * qnw_ref[...]
        qp = qt[base + MLA_NOPE:base + per_head]
        qp = qp * _rms(qp, 0) * qpew_ref[...]
        q1, q2 = qp[:half], qp[half:]
        qh = jnp.concatenate([qn, q1 * cos - q2 * sin, q2 * cos + q1 * sin], axis=0) * qscale
        for dst in (qt_ref, qt8_ref):
            dst[hh, 0:per_head, :] = qh.astype(dst.dtype)
            dst[hh, per_head:QK_DIM, :] = jnp.zeros((QK_DIM - per_head, tm), dst.dtype)

    kvl = pa[:, MLA_Q_RANK:MLA_Q_RANK + MLA_KV_RANK]
    kvn = (kvl * _rms(kvl, 1) * kvlw_ref[...]).astype(bf16)
    kn = jnp.dot(kvn, wuk_ref[...], preferred_element_type=f32)
    vt = lax.dot_general(wuvt_ref[...], kvn, _NT, preferred_element_type=f32)
    for hh in range(MLA_HEADS):
        xh = kn[:, MLA_NOPE * hh:MLA_NOPE * (hh + 1)]
        xh = xh * _rms(xh, 1) * knw_ref[...]
        for dst in (k_ref, k8_ref):
            dst[hh, :, 0:MLA_NOPE] = (xh * MLA_QK_SCALE).astype(dst.dtype)
            dst[hh, :, MLA_NOPE:QK_DIM] = kpe_nat.astype(dst.dtype)
        vt_ref[hh, 0:MLA_V, :] = vt[MLA_V * hh:MLA_V * (hh + 1)].astype(bf16)
        vt_ref[hh, MLA_V:V_ROWS, :] = jnp.ones((V_ROWS - MLA_V, tm), bf16)

    lane = lax.broadcasted_iota(jnp.int32, (tm, LANES), 1)
    lo = lane < NA_HEAD_DIM
    na_scale = NA_HEAD_DIM ** -0.5 * LOG2E
    q_off = MLA_Q_RANK + MLA_KV_RANK
    k_off = q_off + NA_WIDTH
    for p in range(NA_WIDTH // LANES):
        for off, w_ref, dst, scale in ((q_off, naqw_ref, naq_ref, na_scale), (k_off, nakw_ref, nak_ref, 1.0)):
            xp = pa[:, off + LANES * p:off + LANES * (p + 1)]
            sq = xp * xp
            s_lo = jnp.sum(jnp.where(lo, sq, 0.0), axis=1, keepdims=True)
            s_hi = jnp.sum(jnp.where(lo, 0.0, sq), axis=1, keepdims=True)
            r = jnp.where(lo, lax.rsqrt(s_lo / NA_HEAD_DIM + EPS), lax.rsqrt(s_hi / NA_HEAD_DIM + EPS))
            y = xp * r * w_ref[...]
            if scale != 1.0:
                y = y * scale
            dst[:, LANES * p:LANES * (p + 1)] = y.astype(bf16)


def _const_spec(shape):
    nd = len(shape)
    return pl.BlockSpec(shape, lambda i, _nd=nd: (0,) * _nd, pipeline_mode=pl.Buffered(1))


def _project(x, cos_t, sin_t, w, tm):
    n = x.shape[0]
    grid = (n // tm,)
    weights = (w["norm_w"], w["wa"], w["wbt"], w["qlw"], w["kvlw"], w["wuqt"], w["wuk"], w["wuvt"],
               w["qnw"], w["qpew"], w["knw"], w["kpew"], w["naqw"], w["nakw"])
    in_specs = [pl.BlockSpec((tm, D_MODEL), lambda i: (i, 0)),
                pl.BlockSpec((MLA_ROPE // 2, tm), lambda i: (0, i)),
                pl.BlockSpec((MLA_ROPE // 2, tm), lambda i: (0, i))]
    in_specs += [_const_spec(a.shape) for a in weights]
    out_shape = (
        jax.ShapeDtypeStruct((MLA_HEADS, QK_DIM, n), bf16),
        jax.ShapeDtypeStruct((MLA_HEADS, n, QK_DIM), bf16),
        jax.ShapeDtypeStruct((MLA_HEADS, QK_DIM, n), MLA_QK_FP8),
        jax.ShapeDtypeStruct((MLA_HEADS, n, QK_DIM), MLA_QK_FP8),
        jax.ShapeDtypeStruct((MLA_HEADS, V_ROWS, n), bf16),
        jax.ShapeDtypeStruct((MLA_WIDTH, n), bf16),
        jax.ShapeDtypeStruct((n, NA_WIDTH), bf16),
        jax.ShapeDtypeStruct((n, NA_WIDTH), bf16),
        jax.ShapeDtypeStruct((NA_WIDTH, n), bf16),
        jax.ShapeDtypeStruct((NA_WIDTH, n), bf16),
    )
    out_specs = (
        pl.BlockSpec((MLA_HEADS, QK_DIM, tm), lambda i: (0, 0, i)),
        pl.BlockSpec((MLA_HEADS, tm, QK_DIM), lambda i: (0, i, 0)),
        pl.BlockSpec((MLA_HEADS, QK_DIM, tm), lambda i: (0, 0, i)),
        pl.BlockSpec((MLA_HEADS, tm, QK_DIM), lambda i: (0, i, 0)),
        pl.BlockSpec((MLA_HEADS, V_ROWS, tm), lambda i: (0, 0, i)),
        pl.BlockSpec((MLA_WIDTH, tm), lambda i: (0, i)),
        pl.BlockSpec((tm, NA_WIDTH), lambda i: (i, 0)),
        pl.BlockSpec((tm, NA_WIDTH), lambda i: (i, 0)),
        pl.BlockSpec((NA_WIDTH, tm), lambda i: (0, i)),
        pl.BlockSpec((NA_WIDTH, tm), lambda i: (0, i)),
    )
    return pl.pallas_call(
        _proj_kernel,
        grid=grid,
        in_specs=in_specs,
        out_specs=out_specs,
        out_shape=out_shape,
        compiler_params=pltpu.CompilerParams(
            dimension_semantics=("arbitrary",), vmem_limit_bytes=VMEM_LIMIT_BYTES),
        name="proj",
    )(x, cos_t, sin_t, *weights)


def _mla_online(qt, k_ref, vt_ref, km_ref, vtm_ref, acc_ref, l_ref, m_ref, mx_ref, s_ref, tk):
    nk = k_ref.shape[1] // tk
    assert nk % 2 == 0 and nk >= 2

    def scores(i, slot):
        off = pl.multiple_of(i * tk, tk)
        s = jnp.dot(k_ref[0, pl.ds(off, tk), :], qt, preferred_element_type=f32)
        s_ref[slot] = s
        mx_ref[slot] = jnp.max(s, axis=0, keepdims=True)

    def accumulate(i, slot):
        off = pl.multiple_of(i * tk, tk)
        m_prev = m_ref[...]
        m_new = jnp.maximum(m_prev, mx_ref[slot])
        alpha = jnp.exp2(m_prev - m_new)
        p = jnp.exp2(s_ref[slot] - m_new).astype(bf16)
        pv = jnp.dot(vt_ref[0, :, pl.ds(off, tk)], p, preferred_element_type=f32)
        acc_ref[...] = alpha * acc_ref[...] + pv
        m_ref[...] = m_new

    scores(0, 0)

    s = jnp.dot(km_ref[0], qt, preferred_element_type=f32)
    row = lax.broadcasted_iota(jnp.int32, s.shape, 0)
    s = jnp.where(row < N_META, s, NEG)
    m0 = jnp.max(s, axis=0, keepdims=True)
    p = jnp.exp2(s - m0).astype(bf16)
    acc_ref[...] = jnp.dot(vtm_ref[0], p, preferred_element_type=f32)
    m_ref[...] = m0

    def body(j, carry):
        scores(2 * j + 1, 1)
        accumulate(2 * j, 0)
        scores(2 * j + 2, 0)
        accumulate(2 * j + 1, 1)
        return carry

    lax.fori_loop(0, nk // 2 - 1, body, 0)
    scores(nk - 1, 1)
    accumulate(nk - 2, 0)
    accumulate(nk - 1, 1)
    l_ref[...] = acc_ref[MLA_V:MLA_V + 1]


def _mla_fixed_offset(c, qt, k_ref, vt_ref, km_ref, vtm_ref, acc_ref, l_ref, tk):
    nk = k_ref.shape[1] // tk
    assert nk % MLA_TILES_PER_TRIP == 0

    sub = 8

    def col_sums(p):
        return jnp.sum(p.reshape(p.shape[0] // sub, sub, p.shape[1]), axis=0)

    s = jnp.dot(km_ref[0], qt, preferred_element_type=f32)
    row = lax.broadcasted_iota(jnp.int32, s.shape, 0)
    p = jnp.exp2(jnp.where(row < N_META, s - c, NEG))
    acc_ref[0:MLA_V] = jnp.dot(vtm_ref[0, 0:MLA_V], p.astype(bf16), preferred_element_type=f32)
    acc_ref[MLA_V:MLA_V + sub] = col_sums(p)

    def tile(i):
        off = pl.multiple_of(i * tk, tk)
        for qb in range(qt.shape[1] // MXU_DIM):
            cols = slice(qb * MXU_DIM, (qb + 1) * MXU_DIM)
            s = jnp.dot(k_ref[0, pl.ds(off, tk), :], qt[:, cols], preferred_element_type=f32)
            p = jnp.exp2(s - c)
            acc_ref[0:MLA_V, cols] += jnp.dot(vt_ref[0, 0:MLA_V, pl.ds(off, tk)], p.astype(bf16),
                                              preferred_element_type=f32)
            acc_ref[MLA_V:MLA_V + sub, cols] += col_sums(p)

    def body(j, carry):
        for u in range(MLA_TILES_PER_TRIP):
            tile(j * MLA_TILES_PER_TRIP + u)
        return carry

    lax.fori_loop(0, nk // MLA_TILES_PER_TRIP, body, 0)
    l_ref[...] = jnp.sum(acc_ref[MLA_V:MLA_V + sub], axis=0, keepdims=True)


def _mla_kernel(c_ref, qt_ref, k_ref, km_ref, qt8_ref, k8_ref, km8_ref, vt_ref, vtm_ref, g_ref, o_ref,
                acc_ref, l_ref, m_ref, mx_ref, s_ref, *, tk):
    c = c_ref[0]
    bounded = c <= MLA_FIXED_OFFSET_MAX

    @pl.when(bounded)
    def _():
        _mla_fixed_offset(c, qt8_ref[0], k8_ref, vt_ref, km8_ref, vtm_ref, acc_ref, l_ref, tk)

    @pl.when(jnp.logical_not(bounded))
    def _():
        _mla_online(qt_ref[0], k_ref, vt_ref, km_ref, vtm_ref, acc_ref, l_ref, m_ref, mx_ref, s_ref, tk)

    o = acc_ref[0:MLA_V] / l_ref[...]
    g = g_ref[...].astype(f32)
    o_ref[...] = (o * _silu(g)).astype(bf16)


def _mla_score_bound(qn_w, qpe_w, kn_w, kpe_w):
    def sq(nope_w, pe_w):
        return MLA_NOPE * jnp.max(nope_w.astype(f32) ** 2) + MLA_ROPE * jnp.max(pe_w.astype(f32) ** 2)
    margin = (1.0 + 2.0 ** -4) ** 2 * 1.01
    return (margin * MLA_QK_SCALE ** 2 * jnp.sqrt(sq(qn_w, qpe_w) * sq(kn_w, kpe_w))).reshape(1)


def _mla(c, qt, k, km, qt8, k8, km8, vt, vtm, gmt, tq, tk):
    n = qt.shape[2]
    grid = (MLA_HEADS, n // tq)
    return pl.pallas_call(
        functools.partial(_mla_kernel, tk=tk),
        grid=grid,
        in_specs=[
            pl.BlockSpec(memory_space=pltpu.SMEM),
            pl.BlockSpec((1, QK_DIM, tq), lambda h, j: (h, 0, j)),
            pl.BlockSpec((1, n, QK_DIM), lambda h, j: (h, 0, 0)),
            pl.BlockSpec((1, META_PAD, QK_DIM), lambda h, j: (h, 0, 0)),
            pl.BlockSpec((1, QK_DIM, tq), lambda h, j: (h, 0, j)),
            pl.BlockSpec((1, n, QK_DIM), lambda h, j: (h, 0, 0)),
            pl.BlockSpec((1, META_PAD, QK_DIM), lambda h, j: (h, 0, 0)),
            pl.BlockSpec((1, V_ROWS, n), lambda h, j: (h, 0, 0)),
            pl.BlockSpec((1, V_ROWS, META_PAD), lambda h, j: (h, 0, 0)),
            pl.BlockSpec((MLA_V, tq), lambda h, j: (h, j)),
        ],
        out_specs=pl.BlockSpec((MLA_V, tq), lambda h, j: (h, j)),
        out_shape=jax.ShapeDtypeStruct((MLA_WIDTH, n), bf16),
        scratch_shapes=[pltpu.VMEM((V_ROWS, tq), f32), pltpu.VMEM((1, tq), f32), pltpu.VMEM((1, tq), f32),
                        pltpu.VMEM((2, 1, tq), f32), pltpu.VMEM((2, tk, tq), f32)],
        compiler_params=pltpu.CompilerParams(
            dimension_semantics=("arbitrary", "arbitrary"), vmem_limit_bytes=VMEM_LIMIT_BYTES),
        name="mla",
    )(c, qt, k, km, qt8, k8, km8, vt, vtm, gmt)


NA_DR = 2 * NA_WIN_R - 1
NA_DC = 2 * NA_WIN_C - 1


def _na_row_plan():
    r = np.arange(ROWS)
    r0 = np.clip(r - NA_WIN_R // 2, 0, ROWS - NA_WIN_R)
    r0e = np.minimum(r0 - (r0 % 2), ROWS - NA_SLAB_ROWS)
    kr = r0e[:, None] + np.arange(NA_SLAB_ROWS)[None, :]
    in_win = (kr >= r0[:, None]) & (kr < r0[:, None] + NA_WIN_R)
    dr = kr - r[:, None] + (NA_WIN_R - 1)
    blk = np.where(in_win, dr, NA_DR)
    assert blk.min() >= 0 and blk.max() <= NA_DR
    return r0e.astype(np.int32), blk.reshape(-1).astype(np.int32)


def _na_tables(rel_bias, meta_bias):
    c = np.arange(GRID_W)
    c0 = np.clip(c - NA_WIN_C // 2, 0, GRID_W - NA_WIN_C)
    kc = np.arange(GRID_W)
    valid_c = (kc[:, None] >= c0[None, :]) & (kc[:, None] < c0[None, :] + NA_WIN_C)
    dc = kc[:, None] - c[None, :] + (NA_WIN_C - 1)
    sel = (np.arange(NA_DC)[:, None, None] == dc[None]) & valid_c[None]
    toe = jnp.einsum("hrd,dkc->hrkc", rel_bias.astype(f32), jnp.asarray(sel, f32),
                     precision=lax.Precision.HIGHEST)
    t = jnp.where(valid_c[None, None], toe * LOG2E, NEG)
    t = jnp.concatenate([t, jnp.full((NA_HEADS, 1, GRID_W, GRID_W), NEG, f32)], axis=1)
    ng = NA_HEADS // NA_GROUP
    t = t.reshape(ng, NA_GROUP, NA_DR + 1, GRID_W, GRID_W)
    t = t.transpose(0, 2, 3, 1, 4).reshape(ng, NA_DR + 1, GRID_W, NA_GROUP * GRID_W)
    mb = meta_bias.astype(f32) * LOG2E
    mb = jnp.concatenate([mb, jnp.full((NA_HEADS, META_PAD - N_META), NEG, f32)], axis=1)
    tm = jnp.broadcast_to(mb.reshape(ng, NA_GROUP, META_PAD, 1), (ng, NA_GROUP, META_PAD, GRID_W))
    tm = tm.transpose(0, 2, 1, 3).reshape(ng, META_PAD, NA_GROUP * GRID_W)
    return t, tm


def _na_kernel(r0e_ref, blk_ref, q_ref, k_ref, vt_ref, km_ref, vtm_ref, t_ref, tm_ref, g_ref, wmask_ref,
               o_ref, s_ref, mx_ref):
    b = pl.program_id(1)
    gw = NA_GROUP * GRID_W
    lane_lo = lax.broadcasted_iota(jnp.int32, (gw, LANES), 1) < GRID_W
    head_even = (lax.broadcasted_iota(jnp.int32, (gw, LANES), 0) // NA_HEAD_DIM) % 2 == 0
    own_half = lane_lo == head_even
    last = NA_ROWS_PER_STEP - 1

    def scores(rl, slot):
        r = b * NA_ROWS_PER_STEP + rl
        q = q_ref[pl.ds(pl.multiple_of(rl * GRID_W, GRID_W), GRID_W), :]
        wt = jnp.concatenate([q] * NA_GROUP, axis=0) * wmask_ref[...]
        koff = pl.multiple_of(r0e_ref[r] * GRID_W, LANES)
        keys = jnp.concatenate([k_ref[pl.ds(koff, NA_SLAB), :], km_ref[...]], axis=0)
        bias = jnp.concatenate(
            [t_ref[0, blk_ref[r * NA_SLAB_ROWS + j]] for j in range(NA_SLAB_ROWS)] + [tm_ref[0]], axis=0)
        s = lax.dot_general(keys, wt, _NT, preferred_element_type=f32) + bias
        s_ref[slot] = s
        mx_ref[slot] = jnp.max(s, axis=0, keepdims=True)

    def finish(rl, slot):
        r = b * NA_ROWS_PER_STEP + rl
        koff = pl.multiple_of(r0e_ref[r] * GRID_W, LANES)
        p = jnp.exp2(s_ref[slot] - mx_ref[slot])
        l = jnp.sum(p, axis=0, keepdims=True)
        vals = jnp.concatenate([vt_ref[:, pl.ds(koff, NA_SLAB)], vtm_ref[...]], axis=1)
        out = jnp.dot(vals, p.astype(bf16), preferred_element_type=f32)
        rl_ = 1.0 / l
        c = jnp.concatenate([out[:LANES, :LANES] * rl_[:, :LANES], out[LANES:, LANES:] * rl_[:, LANES:]], axis=0)
        return jnp.where(own_half, c, pltpu.roll(c, GRID_W, axis=1))

    scores(0, 0)

    def pair(pi):
        ra = 2 * pi
        scores(ra + 1, 1)
        d0 = finish(ra, 0)
        scores(jnp.minimum(ra + 2, last), 0)
        d1 = finish(ra + 1, 1)
        y = jnp.where(lane_lo, d0, d1)
        loff = pl.multiple_of(pi * LANES, LANES)
        g = g_ref[:, pl.ds(loff, LANES)].astype(f32)
        o_ref[:, pl.ds(loff, LANES)] = (y * _silu(g)).astype(bf16)

    def trip(t, carry):
        for u in range(NA_PAIRS_PER_TRIP):
            pair(t * NA_PAIRS_PER_TRIP + u)
        return carry

    lax.fori_loop(0, NA_ROWS_PER_STEP // (2 * NA_PAIRS_PER_TRIP), trip, 0)


def _na(naq, nak, navt, nakm, navtm, t, tmeta, gnt):
    r0e, blk = _na_row_plan()
    wmask = jnp.asarray(np.kron(np.eye(NA_GROUP), np.ones((GRID_W, NA_HEAD_DIM))), bf16)
    ng = NA_HEADS // NA_GROUP
    gw = NA_GROUP * GRID_W
    tq = NA_ROWS_PER_STEP * GRID_W
    grid = (ng, ROWS // NA_ROWS_PER_STEP)
    grid_spec = pltpu.PrefetchScalarGridSpec(
        num_scalar_prefetch=2,
        grid=grid,
        in_specs=[
            pl.BlockSpec((tq, gw), lambda g, b, *_: (b, g)),
            pl.BlockSpec((SEQ, gw), lambda g, b, *_: (0, g)),
            pl.BlockSpec((gw, SEQ), lambda g, b, *_: (g, 0)),
            pl.BlockSpec((META_PAD, gw), lambda g, b, *_: (0, g)),
            pl.BlockSpec((gw, META_PAD), lambda g, b, *_: (g, 0)),
            pl.BlockSpec((1, NA_DR + 1, GRID_W, gw), lambda g, b, *_: (g, 0, 0, 0)),
            pl.BlockSpec((1, META_PAD, gw), lambda g, b, *_: (g, 0, 0)),
            pl.BlockSpec((gw, tq), lambda g, b, *_: (g, b)),
            pl.BlockSpec((gw, gw), lambda g, b, *_: (0, 0)),
        ],
        out_specs=pl.BlockSpec((gw, tq), lambda g, b, *_: (g, b)),
        scratch_shapes=[pltpu.VMEM((2, NA_KEYS, gw), f32), pltpu.VMEM((2, 1, gw), f32)],
    )
    return pl.pallas_call(
        _na_kernel,
        grid_spec=grid_spec,
        out_shape=jax.ShapeDtypeStruct((NA_WIDTH, SEQ), bf16),
        compiler_params=pltpu.CompilerParams(
            dimension_semantics=("arbitrary", "arbitrary"), vmem_limit_bytes=VMEM_LIMIT_BYTES),
        name="na",
    )(jnp.asarray(r0e), jnp.asarray(blk), naq, nak, navt, nakm, navtm, t, tmeta, gnt, wmask)


def _out_kernel(x_ref, ym_ref, yn_ref, wm_ref, wn_ref, o_ref):
    acc = lax.dot_general(ym_ref[...], wm_ref[...], _TN, preferred_element_type=f32)
    acc = acc + lax.dot_general(yn_ref[...], wn_ref[...], _TN, preferred_element_type=f32)
    o_ref[...] = x_ref[...] + acc


def _out_proj(x, ymt, ynt, wm, wn, tm):
    n = x.shape[0]
    return pl.pallas_call(
        _out_kernel,
        grid=(n // tm,),
        in_specs=[
            pl.BlockSpec((tm, D_MODEL), lambda i: (i, 0)),
            pl.BlockSpec((MLA_WIDTH, tm), lambda i: (0, i)),
            pl.BlockSpec((NA_WIDTH, tm), lambda i: (0, i)),
            _const_spec(wm.shape),
            _const_spec(wn.shape),
        ],
        out_specs=pl.BlockSpec((tm, D_MODEL), lambda i: (i, 0)),
        out_shape=jax.ShapeDtypeStruct((n, D_MODEL), f32),
        compiler_params=pltpu.CompilerParams(
            dimension_semantics=("arbitrary",), vmem_limit_bytes=VMEM_LIMIT_BYTES),
        name="out_proj",
    )(x, ymt, ynt, wm, wn)


def _prep_weights(norm_w, w_in, q_lat_norm_w, kv_lat_norm_w, w_uq, w_ukv,
                  mla_qn_w, mla_qpe_w, mla_kn_w, mla_kpe_w, na_q_norm_w, na_k_norm_w):
    sizes = (MLA_Q_RANK, MLA_KV_RANK, MLA_ROPE, MLA_WIDTH, NA_WIDTH, NA_WIDTH, NA_WIDTH, NA_WIDTH)
    o = np.concatenate([[0], np.cumsum(sizes)])
    seg = [w_in[:, o[i]:o[i + 1]] for i in range(8)]
    wa = jnp.concatenate([seg[0], seg[1], seg[4], seg[5]], axis=1).astype(bf16)
    wbt = jnp.concatenate([seg[3], seg[6], seg[7], seg[2]], axis=1).T.astype(bf16)
    ukv = w_ukv.reshape(MLA_KV_RANK, MLA_HEADS, 2, MLA_NOPE)
    return {
        "norm_w": norm_w.reshape(1, D_MODEL).astype(f32),
        "wa": wa,
        "wbt": wbt,
        "qlw": q_lat_norm_w.reshape(1, MLA_Q_RANK).astype(f32),
        "kvlw": kv_lat_norm_w.reshape(1, MLA_KV_RANK).astype(f32),
        "wuqt": w_uq.T.astype(bf16),
        "wuk": ukv[:, :, 0, :].reshape(MLA_KV_RANK, MLA_HEADS * MLA_NOPE).astype(bf16),
        "wuvt": ukv[:, :, 1, :].reshape(MLA_KV_RANK, MLA_HEADS * MLA_V).T.astype(bf16),
        "qnw": mla_qn_w.reshape(MLA_NOPE, 1).astype(f32),
        "qpew": mla_qpe_w.reshape(MLA_ROPE, 1).astype(f32),
        "knw": mla_kn_w.reshape(1, MLA_NOPE).astype(f32),
        "kpew": mla_kpe_w.reshape(MLA_ROPE, 1).astype(f32),
        "naqw": jnp.tile(na_q_norm_w.reshape(1, NA_HEAD_DIM), (1, 2)).astype(f32),
        "nakw": jnp.tile(na_k_norm_w.reshape(1, NA_HEAD_DIM), (1, 2)).astype(f32),
    }


def kernel(x, meta_tokens, norm_w, w_in, q_lat_norm_w, kv_lat_norm_w, w_uq, w_ukv,
           mla_qn_w, mla_qpe_w, mla_kn_w, mla_kpe_w, na_q_norm_w, na_k_norm_w,
           na_rel_bias, na_meta_bias, w_out):
    assert x.shape == (1, SEQ, D_MODEL) and norm_w.shape[0] == 1
    xr = x[0]
    w = _prep_weights(norm_w[0], w_in[0], q_lat_norm_w[0], kv_lat_norm_w[0], w_uq[0], w_ukv[0],
                      mla_qn_w[0], mla_qpe_w[0], mla_kn_w[0], mla_kpe_w[0],
                      na_q_norm_w[0], na_k_norm_w[0])

    pos = jnp.arange(N_META + SEQ, dtype=f32)
    inv_freq = ROPE_BASE ** (-(jnp.arange(0, MLA_ROPE, 2, dtype=f32) / MLA_ROPE))
    ang = pos[:, None] * inv_freq[None, :]
    cos_t, sin_t = jnp.cos(ang).T, jnp.sin(ang).T
    pad = ((0, 0), (0, META_PAD - N_META))
    cos_m, sin_m = jnp.pad(cos_t[:, :N_META], pad), jnp.pad(sin_t[:, :N_META], pad)

    xm = jnp.pad(meta_tokens.astype(f32), ((0, META_PAD - N_META), (0, 0)))

    qt, k, qt8, k8, vt, gmt, naq, nak, navt, gnt = _project(xr, cos_t[:, N_META:], sin_t[:, N_META:], w, tm=256)
    _, km, _, km8, vtm, _, _, nakm, navtm, _ = _project(xm, cos_m, sin_m, w, tm=META_PAD)

    c = _mla_score_bound(mla_qn_w[0], mla_qpe_w[0], mla_kn_w[0], mla_kpe_w[0])
    ymt = _mla(c, qt, k, km, qt8, k8, km8, vt, vtm, gmt, tq=1024, tk=1024)

    t, tmeta = _na_tables(na_rel_bias[0], na_meta_bias[0])
    ynt = _na(naq, nak, navt, nakm, navtm, t, tmeta, gnt)

    wo = w_out[0].astype(bf16)
    out = _out_proj(xr, ymt, ynt, wo[:MLA_WIDTH], wo[MLA_WIDTH:], tm=512)
    return out[None]
```

```python
import functools
import math

import numpy as np
import jax
import jax.numpy as jnp
from jax import lax
from jax.experimental import pallas as pl
from jax.experimental.pallas import tpu as pltpu

D_MODEL = 1024
SEQ = 16384
N_META = 16
GRID_W = 64
ROWS = SEQ // GRID_W
EPS = 1e-6

MLA_HEADS = 8
MLA_NOPE = 128
MLA_ROPE = 64
MLA_V = 128
MLA_Q_RANK = 256
MLA_KV_RANK = 256
MLA_WIDTH = MLA_HEADS * MLA_V
ROPE_BASE = 10000.0

NA_HEADS = 16
NA_HEAD_DIM = 64
NA_WIDTH = NA_HEADS * NA_HEAD_DIM
NA_WIN_R = 8
NA_WIN_C = 16

LOG2E = math.log2(math.e)
NEG = -1e30

LANES = 128
MXU_DIM = 256
VMEM_LIMIT_BYTES = 56 * 1024 * 1024

META_PAD = 128
QK_DIM = MXU_DIM
V_ROWS = MLA_V + 16

NA_GROUP = 2
NA_SLAB_ROWS = 10
NA_SLAB = NA_SLAB_ROWS * GRID_W
NA_ROWS_PER_STEP = 64
NA_PAIRS_PER_TRIP = 16

_NT = (((1,), (1,)), ((), ()))
_TN = (((0,), (0,)), ((), ()))

f32 = jnp.float32
bf16 = jnp.bfloat16

MLA_QK_FP8 = jnp.float8_e4m3fn
MLA_QK_SCALE = math.sqrt((MLA_NOPE + MLA_ROPE) ** -0.5 * LOG2E)
MLA_FIXED_OFFSET_MAX = 60.0
MLA_TILES_PER_TRIP = 8
NA_FIXED_OFFSET_SPAN = 120.0


def _rms(x, axis):
    return lax.rsqrt(jnp.mean(x * x, axis=axis, keepdims=True) + EPS)


def _silu(g):
    return g / (1.0 + jnp.exp(-g))


def _proj_kernel(x_ref, cos_ref, sin_ref, nw_ref, wa_ref, wbt_ref, qlw_ref, kvlw_ref,
                 wuqt_ref, wuk_ref, wuvt_ref, qnw_ref, qpew_ref, knw_ref, kpew_ref,
                 naqw_ref, nakw_ref,
                 qt_ref, k_ref, qt8_ref, k8_ref, vt_ref, gmt_ref, naq_ref, nak_ref, navt_ref, gnt_ref):
    tm = x_ref.shape[0]
    x = x_ref[...]
    h = (x * _rms(x, 1) * nw_ref[...]).astype(bf16)
    pa = jnp.dot(h, wa_ref[...], preferred_element_type=f32)
    pbt = lax.dot_general(wbt_ref[...], h, _NT, preferred_element_type=f32)

    gmt_ref[...] = pbt[0:MLA_WIDTH].astype(bf16)
    navt_ref[...] = pbt[MLA_WIDTH:MLA_WIDTH + NA_WIDTH].astype(bf16)
    gnt_ref[...] = pbt[MLA_WIDTH + NA_WIDTH:MLA_WIDTH + 2 * NA_WIDTH].astype(bf16)

    cos = cos_ref[...]
    sin = sin_ref[...]
    half = MLA_ROPE // 2

    kpe = pbt[MLA_WIDTH + 2 * NA_WIDTH:MLA_WIDTH + 2 * NA_WIDTH + MLA_ROPE]
    kpe = kpe * _rms(kpe, 0) * kpew_ref[...]
    k1, k2 = kpe[:half], kpe[half:]
    kro = jnp.concatenate(
        [k1 * cos - k2 * sin, k2 * cos + k1 * sin, jnp.zeros((LANES - MLA_ROPE, tm), f32)], axis=0)
    kpe_nat = kro.T * MLA_QK_SCALE

    qscale = MLA_QK_SCALE
    ql = pa[:, 0:MLA_Q_RANK]
    qln = (ql * _rms(ql, 1) * qlw_ref[...]).astype(bf16)
    qt = lax.dot_general(wuqt_ref[...], qln, _NT, preferred_element_type=f32)
    per_head = MLA_NOPE + MLA_ROPE
    for hh in range(MLA_HEADS):
        base = per_head * hh
        qn = qt[base:base + MLA_NOPE]
        qn = qn * _rms(qn, 0) * qnw_ref[...]
        qp = qt[base + MLA_NOPE:base + per_head]
        qp = qp * _rms(qp, 0) * qpew_ref[...]
        q1, q2 = qp[:half], qp[half:]
        qh = jnp.concatenate([qn, q1 * cos - q2 * sin, q2 * cos + q1 * sin], axis=0) * qscale
        for dst in (qt_ref, qt8_ref):
            dst[hh, 0:per_head, :] = qh.astype(dst.dtype)
            dst[hh, per_head:QK_DIM, :] = jnp.zeros((QK_DIM - per_head, tm), dst.dtype)

    kvl = pa[:, MLA_Q_RANK:MLA_Q_RANK + MLA_KV_RANK]
    kvn = (kvl * _rms(kvl, 1) * kvlw_ref[...]).astype(bf16)
    kn = jnp.dot(kvn, wuk_ref[...], preferred_element_type=f32)
    vt = lax.dot_general(wuvt_ref[...], kvn, _NT, preferred_element_type=f32)
    for hh in range(MLA_HEADS):
        xh = kn[:, MLA_NOPE * hh:MLA_NOPE * (hh + 1)]
        xh = xh * _rms(xh, 1) * knw_ref[...]
        for dst in (k_ref, k8_ref):
            dst[hh, :, 0:MLA_NOPE] = (xh * MLA_QK_SCALE).astype(dst.dtype)
            dst[hh, :, MLA_NOPE:QK_DIM] = kpe_nat.astype(dst.dtype)
        vt_ref[hh, 0:MLA_V, :] = vt[MLA_V * hh:MLA_V * (hh + 1)].astype(bf16)
        vt_ref[hh, MLA_V:V_ROWS, :] = jnp.ones((V_ROWS - MLA_V, tm), bf16)

    lane = lax.broadcasted_iota(jnp.int32, (tm, LANES), 1)
    lo = lane < NA_HEAD_DIM
    na_scale = NA_HEAD_DIM ** -0.5 * LOG2E
    q_off = MLA_Q_RANK + MLA_KV_RANK
    k_off = q_off + NA_WIDTH
    for p in range(NA_WIDTH // LANES):
        for off, w_ref, dst, scale in ((q_off, naqw_ref, naq_ref, na_scale), (k_off, nakw_ref, nak_ref, 1.0)):
            xp = pa[:, off + LANES * p:off + LANES * (p + 1)]
            sq = xp * xp
            s_lo = jnp.sum(jnp.where(lo, sq, 0.0), axis=1, keepdims=True)
            s_hi = jnp.sum(jnp.where(lo, 0.0, sq), axis=1, keepdims=True)
            r = jnp.where(lo, lax.rsqrt(s_lo / NA_HEAD_DIM + EPS), lax.rsqrt(s_hi / NA_HEAD_DIM + EPS))
            y = xp * r * w_ref[...]
            if scale != 1.0:
                y = y * scale
            dst[:, LANES * p:LANES * (p + 1)] = y.astype(bf16)


def _const_spec(shape):
    nd = len(shape)
    return pl.BlockSpec(shape, lambda i, _nd=nd: (0,) * _nd, pipeline_mode=pl.Buffered(1))


def _project(x, cos_t, sin_t, w, tm):
    n = x.shape[0]
    grid = (n // tm,)
    weights = (w["norm_w"], w["wa"], w["wbt"], w["qlw"], w["kvlw"], w["wuqt"], w["wuk"], w["wuvt"],
               w["qnw"], w["qpew"], w["knw"], w["kpew"], w["naqw"], w["nakw"])
    in_specs = [pl.BlockSpec((tm, D_MODEL), lambda i: (i, 0)),
                pl.BlockSpec((MLA_ROPE // 2, tm), lambda i: (0, i)),
                pl.BlockSpec((MLA_ROPE // 2, tm), lambda i: (0, i))]
    in_specs += [_const_spec(a.shape) for a in weights]
    out_shape = (
        jax.ShapeDtypeStruct((MLA_HEADS, QK_DIM, n), bf16),
        jax.ShapeDtypeStruct((MLA_HEADS, n, QK_DIM), bf16),
        jax.ShapeDtypeStruct((MLA_HEADS, QK_DIM, n), MLA_QK_FP8),
        jax.ShapeDtypeStruct((MLA_HEADS, n, QK_DIM), MLA_QK_FP8),
        jax.ShapeDtypeStruct((MLA_HEADS, V_ROWS, n), bf16),
        jax.ShapeDtypeStruct((MLA_WIDTH, n), bf16),
        jax.ShapeDtypeStruct((n, NA_WIDTH), bf16),
        jax.ShapeDtypeStruct((n, NA_WIDTH), bf16),
        jax.ShapeDtypeStruct((NA_WIDTH, n), bf16),
        jax.ShapeDtypeStruct((NA_WIDTH, n), bf16),
    )
    out_specs = (
        pl.BlockSpec((MLA_HEADS, QK_DIM, tm), lambda i: (0, 0, i)),
        pl.BlockSpec((MLA_HEADS, tm, QK_DIM), lambda i: (0, i, 0)),
        pl.BlockSpec((MLA_HEADS, QK_DIM, tm), lambda i: (0, 0, i)),
        pl.BlockSpec((MLA_HEADS, tm, QK_DIM), lambda i: (0, i, 0)),
        pl.BlockSpec((MLA_HEADS, V_ROWS, tm), lambda i: (0, 0, i)),
        pl.BlockSpec((MLA_WIDTH, tm), lambda i: (0, i)),
        pl.BlockSpec((tm, NA_WIDTH), lambda i: (i, 0)),
        pl.BlockSpec((tm, NA_WIDTH), lambda i: (i, 0)),
        pl.BlockSpec((NA_WIDTH, tm), lambda i: (0, i)),
        pl.BlockSpec((NA_WIDTH, tm), lambda i: (0, i)),
    )
    return pl.pallas_call(
        _proj_kernel,
        grid=grid,
        in_specs=in_specs,
        out_specs=out_specs,
        out_shape=out_shape,
        compiler_params=pltpu.CompilerParams(
            dimension_semantics=("arbitrary",), vmem_limit_bytes=VMEM_LIMIT_BYTES),
        name="proj",
    )(x, cos_t, sin_t, *weights)


def _mla_online(qt, k_ref, vt_ref, km_ref, vtm_ref, acc_ref, l_ref, m_ref, mx_ref, s_ref, tk):
    nk = k_ref.shape[1] // tk
    assert nk % 2 == 0 and nk >= 2

    def scores(i, slot):
        off = pl.multiple_of(i * tk, tk)
        s = jnp.dot(k_ref[0, pl.ds(off, tk), :], qt, preferred_element_type=f32)
        s_ref[slot] = s
        mx_ref[slot] = jnp.max(s, axis=0, keepdims=True)

    def accumulate(i, slot):
        off = pl.multiple_of(i * tk, tk)
        m_prev = m_ref[...]
        m_new = jnp.maximum(m_prev, mx_ref[slot])
        alpha = jnp.exp2(m_prev - m_new)
        p = jnp.exp2(s_ref[slot] - m_new).astype(bf16)
        pv = jnp.dot(vt_ref[0, :, pl.ds(off, tk)], p, preferred_element_type=f32)
        acc_ref[...] = alpha * acc_ref[...] + pv
        m_ref[...] = m_new

    scores(0, 0)

    s = jnp.dot(km_ref[0], qt, preferred_element_type=f32)
    row = lax.broadcasted_iota(jnp.int32, s.shape, 0)
    s = jnp.where(row < N_META, s, NEG)
    m0 = jnp.max(s, axis=0, keepdims=True)
    p = jnp.exp2(s - m0).astype(bf16)
    acc_ref[...] = jnp.dot(vtm_ref[0], p, preferred_element_type=f32)
    m_ref[...] = m0

    def body(j, carry):
        scores(2 * j + 1, 1)
        accumulate(2 * j, 0)
        scores(2 * j + 2, 0)
        accumulate(2 * j + 1, 1)
        return carry

    lax.fori_loop(0, nk // 2 - 1, body, 0)
    scores(nk - 1, 1)
    accumulate(nk - 2, 0)
    accumulate(nk - 1, 1)
    l_ref[...] = acc_ref[MLA_V:MLA_V + 1]


def _mla_fixed_offset(c, qt, k_ref, vt_ref, km_ref, vtm_ref, acc_ref, l_ref, tk):
    nk = k_ref.shape[1] // tk
    assert nk % MLA_TILES_PER_TRIP == 0

    sub = 8

    def col_sums(p):
        return jnp.sum(p.reshape(p.shape[0] // sub, sub, p.shape[1]), axis=0)

    s = jnp.dot(km_ref[0], qt, preferred_element_type=f32)
    row = lax.broadcasted_iota(jnp.int32, s.shape, 0)
    p = jnp.exp2(jnp.where(row < N_META, s - c, NEG))
    acc_ref[0:MLA_V] = jnp.dot(vtm_ref[0, 0:MLA_V], p.astype(bf16), preferred_element_type=f32)
    acc_ref[MLA_V:MLA_V + sub] = col_sums(p)

    def tile(i):
        off = pl.multiple_of(i * tk, tk)
        for qb in range(qt.shape[1] // MXU_DIM):
            cols = slice(qb * MXU_DIM, (qb + 1) * MXU_DIM)
            s = jnp.dot(k_ref[0, pl.ds(off, tk), :], qt[:, cols], preferred_element_type=f32)
            p = jnp.exp2(s - c)
            acc_ref[0:MLA_V, cols] += jnp.dot(vt_ref[0, 0:MLA_V, pl.ds(off, tk)], p.astype(bf16),
                                              preferred_element_type=f32)
            acc_ref[MLA_V:MLA_V + sub, cols] += col_sums(p)

    def body(j, carry):
        for u in range(MLA_TILES_PER_TRIP):
            tile(j * MLA_TILES_PER_TRIP + u)
        return carry

    lax.fori_loop(0, nk // MLA_TILES_PER_TRIP, body, 0)
    l_ref[...] = jnp.sum(acc_ref[MLA_V:MLA_V + sub], axis=0, keepdims=True)


def _mla_kernel(c_ref, qt_ref, k_ref, km_ref, qt8_ref, k8_ref, km8_ref, vt_ref, vtm_ref, g_ref, o_ref,
                acc_ref, l_ref, m_ref, mx_ref, s_ref, *, tk):
    c = c_ref[0]
    bounded = c <= MLA_FIXED_OFFSET_MAX

    @pl.when(bounded)
    def _():
        _mla_fixed_offset(c, qt8_ref[0], k8_ref, vt_ref, km8_ref, vtm_ref, acc_ref, l_ref, tk)

    @pl.when(jnp.logical_not(bounded))
    def _():
        _mla_online(qt_ref[0], k_ref, vt_ref, km_ref, vtm_ref, acc_ref, l_ref, m_ref, mx_ref, s_ref, tk)

    o = acc_ref[0:MLA_V] / l_ref[...]
    g = g_ref[...].astype(f32)
    o_ref[...] = (o * _silu(g)).astype(bf16)


def _mla_score_bound(qn_w, qpe_w, kn_w, kpe_w):
    def sq(nope_w, pe_w):
        return MLA_NOPE * jnp.max(nope_w.astype(f32) ** 2) + MLA_ROPE * jnp.max(pe_w.astype(f32) ** 2)
    margin = (1.0 + 2.0 ** -4) ** 2 * 1.01
    return (margin * MLA_QK_SCALE ** 2 * jnp.sqrt(sq(qn_w, qpe_w) * sq(kn_w, kpe_w))).reshape(1)


def _mla(c, qt, k, km, qt8, k8, km8, vt, vtm, gmt, tq, tk):
    n = qt.shape[2]
    grid = (MLA_HEADS, n // tq)
    return pl.pallas_call(
        functools.partial(_mla_kernel, tk=tk),
        grid=grid,
        in_specs=[
            pl.BlockSpec(memory_space=pltpu.SMEM),
            pl.BlockSpec((1, QK_DIM, tq), lambda h, j: (h, 0, j)),
            pl.BlockSpec((1, n, QK_DIM), lambda h, j: (h, 0, 0)),
            pl.BlockSpec((1, META_PAD, QK_DIM), lambda h, j: (h, 0, 0)),
            pl.BlockSpec((1, QK_DIM, tq), lambda h, j: (h, 0, j)),
            pl.BlockSpec((1, n, QK_DIM), lambda h, j: (h, 0, 0)),
            pl.BlockSpec((1, META_PAD, QK_DIM), lambda h, j: (h, 0, 0)),
            pl.BlockSpec((1, V_ROWS, n), lambda h, j: (h, 0, 0)),
            pl.BlockSpec((1, V_ROWS, META_PAD), lambda h, j: (h, 0, 0)),
            pl.BlockSpec((MLA_V, tq), lambda h, j: (h, j)),
        ],
        out_specs=pl.BlockSpec((MLA_V, tq), lambda h, j: (h, j)),
        out_shape=jax.ShapeDtypeStruct((MLA_WIDTH, n), bf16),
        scratch_shapes=[pltpu.VMEM((V_ROWS, tq), f32), pltpu.VMEM((1, tq), f32), pltpu.VMEM((1, tq), f32),
                        pltpu.VMEM((2, 1, tq), f32), pltpu.VMEM((2, tk, tq), f32)],
        compiler_params=pltpu.CompilerParams(
            dimension_semantics=("arbitrary", "arbitrary"), vmem_limit_bytes=VMEM_LIMIT_BYTES),
        name="mla",
    )(c, qt, k, km, qt8, k8, km8, vt, vtm, gmt)


NA_DR = 2 * NA_WIN_R - 1
NA_DC = 2 * NA_WIN_C - 1


def _na_pair_plan():
    r = np.arange(ROWS)
    r0 = np.clip(r - NA_WIN_R // 2, 0, ROWS - NA_WIN_R)
    r0e = np.minimum(r0 - (r0 % 2), ROWS - NA_SLAB_ROWS)[0::2]
    r0e_row = np.repeat(r0e, 2)
    kr = r0e_row[:, None] + np.arange(NA_SLAB_ROWS)[None, :]
    in_win = (kr >= r0[:, None]) & (kr < r0[:, None] + NA_WIN_R)
    assert (in_win.sum(axis=1) == NA_WIN_R).all()
    dr = kr - r[:, None] + (NA_WIN_R - 1)
    blk = np.where(in_win, dr, NA_DR)
    assert blk.min() >= 0 and blk.max() <= NA_DR
    return r0e.astype(np.int32), blk.reshape(-1).astype(np.int32)


def _na_score_bounds(q_w, k_w, rel_bias, meta_bias):
    qk = 1.02 * NA_HEAD_DIM ** -0.5 * LOG2E * NA_HEAD_DIM * jnp.max(jnp.abs(q_w.astype(f32))) * jnp.max(
        jnp.abs(k_w.astype(f32)))
    b_hi = LOG2E * jnp.maximum(jnp.max(rel_bias), jnp.max(meta_bias)).astype(f32)
    b_lo = LOG2E * jnp.minimum(jnp.min(rel_bias), jnp.min(meta_bias)).astype(f32)
    bounded = 2.0 * qk + (b_hi - b_lo) <= NA_FIXED_OFFSET_SPAN
    return qk + b_hi, bounded


def _na_tables(rel_bias, meta_bias, shift):
    c = np.arange(GRID_W)
    c0 = np.clip(c - NA_WIN_C // 2, 0, GRID_W - NA_WIN_C)
    kc = np.arange(GRID_W)
    valid_c = (kc[:, None] >= c0[None, :]) & (kc[:, None] < c0[None, :] + NA_WIN_C)
    dc = kc[:, None] - c[None, :] + (NA_WIN_C - 1)
    sel = (np.arange(NA_DC)[:, None, None] == dc[None]) & valid_c[None]
    toe = jnp.einsum("hrd,dkc->hrkc", rel_bias.astype(f32), jnp.asarray(sel, f32),
                     precision=lax.Precision.HIGHEST)
    t = jnp.where(valid_c[None, None], toe * LOG2E - shift, NEG)
    t = jnp.concatenate([t, jnp.full((NA_HEADS, 1, GRID_W, GRID_W), NEG, f32)], axis=1)
    ng = NA_HEADS // NA_GROUP
    t = t.reshape(ng, NA_GROUP, NA_DR + 1, GRID_W, GRID_W)
    t = t.transpose(0, 2, 3, 1, 4).reshape(ng, NA_DR + 1, GRID_W, NA_GROUP * GRID_W)
    mb = meta_bias.astype(f32) * LOG2E - shift
    mb = jnp.concatenate([mb, jnp.full((NA_HEADS, META_PAD - N_META), NEG, f32)], axis=1)
    tm = jnp.broadcast_to(mb.reshape(ng, NA_GROUP, META_PAD, 1), (ng, NA_GROUP, META_PAD, GRID_W))
    tm = tm.transpose(0, 2, 1, 3).reshape(ng, META_PAD, NA_GROUP * GRID_W)
    return t, tm


def _na_kernel(r0e_ref, blk_ref, flag_ref, q_ref, k_ref, vt_ref, km_ref, vtm_ref, t_ref, tm_ref, g_ref,
               wmask_ref, o_ref, s_ref):
    b = pl.program_id(1)
    gw = NA_GROUP * GRID_W
    lane_lo = lax.broadcasted_iota(jnp.int32, (gw, LANES), 1) < GRID_W
    head_even = lax.broadcasted_iota(jnp.int32, (gw, LANES), 0) < NA_HEAD_DIM
    pairs_per_step = NA_ROWS_PER_STEP // 2

    def scores(pi, slot):
        pg = b * pairs_per_step + pi
        rows = pl.ds(pl.multiple_of(pi * 2 * GRID_W, 2 * GRID_W), 2 * GRID_W)
        q2 = q_ref[rows, :]
        qa, qb = q2[:GRID_W], q2[GRID_W:]
        wt = jnp.concatenate([qa] * NA_GROUP + [qb] * NA_GROUP, axis=0) * wmask_ref[...]
        koff = pl.multiple_of(r0e_ref[pg] * GRID_W, LANES)
        keys = jnp.concatenate([k_ref[pl.ds(koff, NA_SLAB), :], km_ref[...]], axis=0)

        def table(rr):
            base = (pg * 2 + rr) * NA_SLAB_ROWS
            return jnp.concatenate([t_ref[0, blk_ref[base + j]] for j in range(NA_SLAB_ROWS)] + [tm_ref[0]], axis=0)

        bias = jnp.concatenate([table(0), table(1)], axis=1)
        s_ref[slot] = lax.dot_general(keys, wt, _NT, preferred_element_type=f32) + bias

    def finish(pi, slot, use_max):
        pg = b * pairs_per_step + pi
        koff = pl.multiple_of(r0e_ref[pg] * GRID_W, LANES)
        s = s_ref[slot]
        if use_max:
            s = s - jnp.max(s, axis=0, keepdims=True)
        p = jnp.exp2(s)
        l = jnp.sum(p, axis=0, keepdims=True)
        vals = jnp.concatenate([vt_ref[:, pl.ds(koff, NA_SLAB)], vtm_ref[...]], axis=1)
        out = jnp.dot(vals, p.astype(bf16), preferred_element_type=f32) * (1.0 / l)
        oa, ob = out[:, :LANES], out[:, LANES:]
        ya = jnp.where(head_even, oa, pltpu.roll(oa, GRID_W, axis=1))
        yb = jnp.where(head_even, pltpu.roll(ob, GRID_W, axis=1), ob)
        y = jnp.where(lane_lo, ya, yb)
        loff = pl.multiple_of(pi * LANES, LANES)
        g = g_ref[:, pl.ds(loff, LANES)].astype(f32)
        o_ref[:, pl.ds(loff, LANES)] = (y * _silu(g)).astype(bf16)

    def sweep(use_max):
        assert NA_PAIRS_PER_TRIP % 2 == 0
        scores(0, 0)

        def trip(t, carry):
            for u in range(NA_PAIRS_PER_TRIP):
                pi = t * NA_PAIRS_PER_TRIP + u
                scores(jnp.minimum(pi + 1, pairs_per_step - 1), (u + 1) % 2)
                finish(pi, u % 2, use_max)
            return carry
        lax.fori_loop(0, pairs_per_step // NA_PAIRS_PER_TRIP, trip, 0)

    bounded = flag_ref[0] != 0

    @pl.when(bounded)
    def _():
        sweep(False)

    @pl.when(jnp.logical_not(bounded))
    def _():
        sweep(True)


def _na(naq, nak, navt, nakm, navtm, t, tmeta, gnt, bounded):
    r0e, blk = _na_pair_plan()
    ng = NA_HEADS // NA_GROUP
    gw = NA_GROUP * GRID_W
    assert gw == LANES
    wmask = jnp.asarray(np.tile(np.kron(np.eye(NA_GROUP), np.ones((GRID_W, NA_HEAD_DIM))), (2, 1)), bf16)
    tq = NA_ROWS_PER_STEP * GRID_W
    grid = (ng, ROWS // NA_ROWS_PER_STEP)
    grid_spec = pltpu.PrefetchScalarGridSpec(
        num_scalar_prefetch=3,
        grid=grid,
        in_specs=[
            pl.BlockSpec((tq, gw), lambda g, b, *_: (b, g)),
            pl.BlockSpec((SEQ, gw), lambda g, b, *_: (0, g)),
            pl.BlockSpec((gw, SEQ), lambda g, b, *_: (g, 0)),
            pl.BlockSpec((META_PAD, gw), lambda g, b, *_: (0, g)),
            pl.BlockSpec((gw, META_PAD), lambda g, b, *_: (g, 0)),
            pl.BlockSpec((1, NA_DR + 1, GRID_W, gw), lambda g, b, *_: (g, 0, 0, 0)),
            pl.BlockSpec((1, META_PAD, gw), lambda g, b, *_: (g, 0, 0)),
            pl.BlockSpec((gw, tq), lambda g, b, *_: (g, b)),
            pl.BlockSpec((2 * gw, gw), lambda g, b, *_: (0, 0)),
        ],
        out_specs=pl.BlockSpec((gw, tq), lambda g, b, *_: (g, b)),
        scratch_shapes=[pltpu.VMEM((2, NA_SLAB + META_PAD, 2 * gw), f32)],
    )
    return pl.pallas_call(
        _na_kernel,
        grid_spec=grid_spec,
        out_shape=jax.ShapeDtypeStruct((NA_WIDTH, SEQ), bf16),
        compiler_params=pltpu.CompilerParams(
            dimension_semantics=("arbitrary", "arbitrary"), vmem_limit_bytes=VMEM_LIMIT_BYTES),
        name="na",
    )(jnp.asarray(r0e), jnp.asarray(blk), bounded.astype(jnp.int32).reshape(1),
      naq, nak, navt, nakm, navtm, t, tmeta, gnt, wmask)


def _out_kernel(x_ref, ym_ref, yn_ref, wm_ref, wn_ref, o_ref):
    acc = lax.dot_general(ym_ref[...], wm_ref[...], _TN, preferred_element_type=f32)
    acc = acc + lax.dot_general(yn_ref[...], wn_ref[...], _TN, preferred_element_type=f32)
    o_ref[...] = x_ref[...] + acc


def _out_proj(x, ymt, ynt, wm, wn, tm):
    n = x.shape[0]
    return pl.pallas_call(
        _out_kernel,
        grid=(n // tm,),
        in_specs=[
            pl.BlockSpec((tm, D_MODEL), lambda i: (i, 0)),
            pl.BlockSpec((MLA_WIDTH, tm), lambda i: (0, i)),
            pl.BlockSpec((NA_WIDTH, tm), lambda i: (0, i)),
            _const_spec(wm.shape),
            _const_spec(wn.shape),
        ],
        out_specs=pl.BlockSpec((tm, D_MODEL), lambda i: (i, 0)),
        out_shape=jax.ShapeDtypeStruct((n, D_MODEL), f32),
        compiler_params=pltpu.CompilerParams(
            dimension_semantics=("arbitrary",), vmem_limit_bytes=VMEM_LIMIT_BYTES),
        name="out_proj",
    )(x, ymt, ynt, wm, wn)


def _prep_weights(norm_w, w_in, q_lat_norm_w, kv_lat_norm_w, w_uq, w_ukv,
                  mla_qn_w, mla_qpe_w, mla_kn_w, mla_kpe_w, na_q_norm_w, na_k_norm_w):
    sizes = (MLA_Q_RANK, MLA_KV_RANK, MLA_ROPE, MLA_WIDTH, NA_WIDTH, NA_WIDTH, NA_WIDTH, NA_WIDTH)
    o = np.concatenate([[0], np.cumsum(sizes)])
    seg = [w_in[:, o[i]:o[i + 1]] for i in range(8)]
    wa = jnp.concatenate([seg[0], seg[1], seg[4], seg[5]], axis=1).astype(bf16)
    wbt = jnp.concatenate([seg[3], seg[6], seg[7], seg[2]], axis=1).T.astype(bf16)
    ukv = w_ukv.reshape(MLA_KV_RANK, MLA_HEADS, 2, MLA_NOPE)
    return {
        "norm_w": norm_w.reshape(1, D_MODEL).astype(f32),
        "wa": wa,
        "wbt": wbt,
        "qlw": q_lat_norm_w.reshape(1, MLA_Q_RANK).astype(f32),
        "kvlw": kv_lat_norm_w.reshape(1, MLA_KV_RANK).astype(f32),
        "wuqt": w_uq.T.astype(bf16),
        "wuk": ukv[:, :, 0, :].reshape(MLA_KV_RANK, MLA_HEADS * MLA_NOPE).astype(bf16),
        "wuvt": ukv[:, :, 1, :].reshape(MLA_KV_RANK, MLA_HEADS * MLA_V).T.astype(bf16),
        "qnw": mla_qn_w.reshape(MLA_NOPE, 1).astype(f32),
        "qpew": mla_qpe_w.reshape(MLA_ROPE, 1).astype(f32),
        "knw": mla_kn_w.reshape(1, MLA_NOPE).astype(f32),
        "kpew": mla_kpe_w.reshape(MLA_ROPE, 1).astype(f32),
        "naqw": jnp.tile(na_q_norm_w.reshape(1, NA_HEAD_DIM), (1, 2)).astype(f32),
        "nakw": jnp.tile(na_k_norm_w.reshape(1, NA_HEAD_DIM), (1, 2)).astype(f32),
    }


def kernel(x, meta_tokens, norm_w, w_in, q_lat_norm_w, kv_lat_norm_w, w_uq, w_ukv,
           mla_qn_w, mla_qpe_w, mla_kn_w, mla_kpe_w, na_q_norm_w, na_k_norm_w,
           na_rel_bias, na_meta_bias, w_out):
    assert x.shape == (1, SEQ, D_MODEL) and norm_w.shape[0] == 1
    xr = x[0]
    w = _prep_weights(norm_w[0], w_in[0], q_lat_norm_w[0], kv_lat_norm_w[0], w_uq[0], w_ukv[0],
                      mla_qn_w[0], mla_qpe_w[0], mla_kn_w[0], mla_kpe_w[0],
                      na_q_norm_w[0], na_k_norm_w[0])

    pos = jnp.arange(N_META + SEQ, dtype=f32)
    inv_freq = ROPE_BASE ** (-(jnp.arange(0, MLA_ROPE, 2, dtype=f32) / MLA_ROPE))
    ang = pos[:, None] * inv_freq[None, :]
    cos_t, sin_t = jnp.cos(ang).T, jnp.sin(ang).T
    pad = ((0, 0), (0, META_PAD - N_META))
    cos_m, sin_m = jnp.pad(cos_t[:, :N_META], pad), jnp.pad(sin_t[:, :N_META], pad)

    xm = jnp.pad(meta_tokens.astype(f32), ((0, META_PAD - N_META), (0, 0)))

    qt, k, qt8, k8, vt, gmt, naq, nak, navt, gnt = _project(xr, cos_t[:, N_META:], sin_t[:, N_META:], w, tm=256)
    _, km, _, km8, vtm, _, _, nakm, navtm, _ = _project(xm, cos_m, sin_m, w, tm=META_PAD)

    c = _mla_score_bound(mla_qn_w[0], mla_qpe_w[0], mla_kn_w[0], mla_kpe_w[0])
    ymt = _mla(c, qt, k, km, qt8, k8, km8, vt, vtm, gmt, tq=1024, tk=1024)

    na_off, na_bounded = _na_score_bounds(na_q_norm_w[0], na_k_norm_w[0], na_rel_bias[0], na_meta_bias[0])
    t, tmeta = _na_tables(na_rel_bias[0], na_meta_bias[0], jnp.where(na_bounded, na_off, 0.0))
    ynt = _na(naq, nak, navt, nakm, navtm, t, tmeta, gnt, na_bounded)

    wo = w_out[0].astype(bf16)
    out = _out_proj(xr, ymt, ynt, wo[:MLA_WIDTH], wo[MLA_WIDTH:], tm=512)
    return out[None]
```

```python
import functools
import math

import numpy as np
import jax
import jax.numpy as jnp
from jax import lax
from jax.experimental import pallas as pl
from jax.experimental.pallas import tpu as pltpu

D_MODEL = 1024
SEQ = 16384
N_META = 16
GRID_W = 64
ROWS = SEQ // GRID_W
EPS = 1e-6

MLA_HEADS = 8
MLA_NOPE = 128
MLA_ROPE = 64
MLA_V = 128
MLA_Q_RANK = 256
MLA_KV_RANK = 256
MLA_WIDTH = MLA_HEADS * MLA_V
ROPE_BASE = 10000.0

NA_HEADS = 16
NA_HEAD_DIM = 64
NA_WIDTH = NA_HEADS * NA_HEAD_DIM
NA_WIN_R = 8
NA_WIN_C = 16

LOG2E = math.log2(math.e)
NEG = -1e30

LANES = 128
MXU_DIM = 256
VMEM_LIMIT_BYTES = 56 * 1024 * 1024

META_PAD = 128
QK_DIM = MXU_DIM
V_ROWS = MLA_V + 16

NA_GROUP = 2
NA_SLAB_ROWS = 10
NA_SLAB = NA_SLAB_ROWS * GRID_W
NA_ROWS_PER_STEP = 128
NA_PAIRS_PER_TRIP = 16

_NT = (((1,), (1,)), ((), ()))
_TN = (((0,), (0,)), ((), ()))

f32 = jnp.float32
bf16 = jnp.bfloat16

MLA_QK_FP8 = jnp.float8_e4m3fn
MLA_QK_SCALE = math.sqrt((MLA_NOPE + MLA_ROPE) ** -0.5 * LOG2E)
MLA_FIXED_OFFSET_MAX = 60.0
MLA_TILES_PER_TRIP = 8
NA_FIXED_OFFSET_SPAN = 120.0


def _rms(x, axis):
    return lax.rsqrt(jnp.mean(x * x, axis=axis, keepdims=True) + EPS)


def _silu(g):
    return g / (1.0 + jnp.exp(-g))


def _proj_kernel(x_ref, cos_ref, sin_ref, nw_ref, wa_ref, wbt_ref, qlw_ref, kvlw_ref,
                 wuqt_ref, wuk_ref, wuvt_ref, qnw_ref, qpew_ref, knw_ref, kpew_ref,
                 naqw_ref, nakw_ref,
                 qt_ref, k_ref, qt8_ref, k8_ref, vt_ref, gmt_ref, naq_ref, nak_ref, navt_ref, gnt_ref):
    tm = x_ref.shape[0]
    x = x_ref[...]
    h = (x * _rms(x, 1) * nw_ref[...]).astype(bf16)
    pa = jnp.dot(h, wa_ref[...], preferred_element_type=f32)
    pbt = lax.dot_general(wbt_ref[...], h, _NT, preferred_element_type=f32)

    gmt_ref[...] = pbt[0:MLA_WIDTH].astype(bf16)
    navt_ref[...] = pbt[MLA_WIDTH:MLA_WIDTH + NA_WIDTH].astype(bf16)
    gnt_ref[...] = pbt[MLA_WIDTH + NA_WIDTH:MLA_WIDTH + 2 * NA_WIDTH].astype(bf16)

    cos = cos_ref[...]
    sin = sin_ref[...]
    half = MLA_ROPE // 2

    kpe = pbt[MLA_WIDTH + 2 * NA_WIDTH:MLA_WIDTH + 2 * NA_WIDTH + MLA_ROPE]
    kpe = kpe * _rms(kpe, 0) * kpew_ref[...]
    k1, k2 = kpe[:half], kpe[half:]
    kro = jnp.concatenate(
        [k1 * cos - k2 * sin, k2 * cos + k1 * sin, jnp.zeros((LANES - MLA_ROPE, tm), f32)], axis=0)
    kpe_nat = kro.T * MLA_QK_SCALE

    qscale = MLA_QK_SCALE
    ql = pa[:, 0:MLA_Q_RANK]
    qln = (ql * _rms(ql, 1) * qlw_ref[...]).astype(bf16)
    qt = lax.dot_general(wuqt_ref[...], qln, _NT, preferred_element_type=f32)
    per_head = MLA_NOPE + MLA_ROPE
    for hh in range(MLA_HEADS):
        base = per_head * hh
        qn = qt[base:base + MLA_NOPE]
        qn = qn * _rms(qn, 0) * qnw_ref[...]
        qp = qt[base + MLA_NOPE:base + per_head]
        qp = qp * _rms(qp, 0) * qpew_ref[...]
        q1, q2 = qp[:half], qp[half:]
        qh = jnp.concatenate([qn, q1 * cos - q2 * sin, q2 * cos + q1 * sin], axis=0) * qscale
        for dst in (qt_ref, qt8_ref):
            dst[hh, 0:per_head, :] = qh.astype(dst.dtype)
            dst[hh, per_head:QK_DIM, :] = jnp.zeros((QK_DIM - per_head, tm), dst.dtype)

    kvl = pa[:, MLA_Q_RANK:MLA_Q_RANK + MLA_KV_RANK]
    kvn = (kvl * _rms(kvl, 1) * kvlw_ref[...]).astype(bf16)
    kn = jnp.dot(kvn, wuk_ref[...], preferred_element_type=f32)
    vt = lax.dot_general(wuvt_ref[...], kvn, _NT, preferred_element_type=f32)
    for hh in range(MLA_HEADS):
        xh = kn[:, MLA_NOPE * hh:MLA_NOPE * (hh + 1)]
        xh = xh * _rms(xh, 1) * knw_ref[...]
        for dst in (k_ref, k8_ref):
            dst[hh, :, 0:MLA_NOPE] = (xh * MLA_QK_SCALE).astype(dst.dtype)
            dst[hh, :, MLA_NOPE:QK_DIM] = kpe_nat.astype(dst.dtype)
        vt_ref[hh, 0:MLA_V, :] = vt[MLA_V * hh:MLA_V * (hh + 1)].astype(bf16)
        vt_ref[hh, MLA_V:V_ROWS, :] = jnp.ones((V_ROWS - MLA_V, tm), bf16)

    lane = lax.broadcasted_iota(jnp.int32, (tm, LANES), 1)
    lo = lane < NA_HEAD_DIM
    na_scale = NA_HEAD_DIM ** -0.5 * LOG2E
    q_off = MLA_Q_RANK + MLA_KV_RANK
    k_off = q_off + NA_WIDTH
    for p in range(NA_WIDTH // LANES):
        for off, w_ref, dst, scale in ((q_off, naqw_ref, naq_ref, na_scale), (k_off, nakw_ref, nak_ref, 1.0)):
            xp = pa[:, off + LANES * p:off + LANES * (p + 1)]
            sq = xp * xp
            s_lo = jnp.sum(jnp.where(lo, sq, 0.0), axis=1, keepdims=True)
            s_hi = jnp.sum(jnp.where(lo, 0.0, sq), axis=1, keepdims=True)
            r = jnp.where(lo, lax.rsqrt(s_lo / NA_HEAD_DIM + EPS), lax.rsqrt(s_hi / NA_HEAD_DIM + EPS))
            y = xp * r * w_ref[...]
            if scale != 1.0:
                y = y * scale
            dst[:, LANES * p:LANES * (p + 1)] = y.astype(bf16)


def _const_spec(shape):
    nd = len(shape)
    return pl.BlockSpec(shape, lambda i, _nd=nd: (0,) * _nd, pipeline_mode=pl.Buffered(1))


def _project(x, cos_t, sin_t, w, tm):
    n = x.shape[0]
    grid = (n // tm,)
    weights = (w["norm_w"], w["wa"], w["wbt"], w["qlw"], w["kvlw"], w["wuqt"], w["wuk"], w["wuvt"],
               w["qnw"], w["qpew"], w["knw"], w["kpew"], w["naqw"], w["nakw"])
    in_specs = [pl.BlockSpec((tm, D_MODEL), lambda i: (i, 0)),
                pl.BlockSpec((MLA_ROPE // 2, tm), lambda i: (0, i)),
                pl.BlockSpec((MLA_ROPE // 2, tm), lambda i: (0, i))]
    in_specs += [_const_spec(a.shape) for a in weights]
    out_shape = (
        jax.ShapeDtypeStruct((MLA_HEADS, QK_DIM, n), bf16),
        jax.ShapeDtypeStruct((MLA_HEADS, n, QK_DIM), bf16),
        jax.ShapeDtypeStruct((MLA_HEADS, QK_DIM, n), MLA_QK_FP8),
        jax.ShapeDtypeStruct((MLA_HEADS, n, QK_DIM), MLA_QK_FP8),
        jax.ShapeDtypeStruct((MLA_HEADS, V_ROWS, n), bf16),
        jax.ShapeDtypeStruct((MLA_WIDTH, n), bf16),
        jax.ShapeDtypeStruct((n, NA_WIDTH), bf16),
        jax.ShapeDtypeStruct((n, NA_WIDTH), bf16),
        jax.ShapeDtypeStruct((NA_WIDTH, n), bf16),
        jax.ShapeDtypeStruct((NA_WIDTH, n), bf16),
    )
    out_specs = (
        pl.BlockSpec((MLA_HEADS, QK_DIM, tm), lambda i: (0, 0, i)),
        pl.BlockSpec((MLA_HEADS, tm, QK_DIM), lambda i: (0, i, 0)),
        pl.BlockSpec((MLA_HEADS, QK_DIM, tm), lambda i: (0, 0, i)),
        pl.BlockSpec((MLA_HEADS, tm, QK_DIM), lambda i: (0, i, 0)),
        pl.BlockSpec((MLA_HEADS, V_ROWS, tm), lambda i: (0, 0, i)),
        pl.BlockSpec((MLA_WIDTH, tm), lambda i: (0, i)),
        pl.BlockSpec((tm, NA_WIDTH), lambda i: (i, 0)),
        pl.BlockSpec((tm, NA_WIDTH), lambda i: (i, 0)),
        pl.BlockSpec((NA_WIDTH, tm), lambda i: (0, i)),
        pl.BlockSpec((NA_WIDTH, tm), lambda i: (0, i)),
    )
    return pl.pallas_call(
        _proj_kernel,
        grid=grid,
        in_specs=in_specs,
        out_specs=out_specs,
        out_shape=out_shape,
        compiler_params=pltpu.CompilerParams(
            dimension_semantics=("arbitrary",), vmem_limit_bytes=VMEM_LIMIT_BYTES),
        name="proj",
    )(x, cos_t, sin_t, *weights)


def _mla_online(qt, k_ref, vt_ref, km_ref, vtm_ref, acc_ref, l_ref, m_ref, mx_ref, s_ref, tk):
    nk = k_ref.shape[1] // tk
    assert nk % 2 == 0 and nk >= 2

    def scores(i, slot):
        off = pl.multiple_of(i * tk, tk)
        s = jnp.dot(k_ref[0, pl.ds(off, tk), :], qt, preferred_element_type=f32)
        s_ref[slot] = s
        mx_ref[slot] = jnp.max(s, axis=0, keepdims=True)

    def accumulate(i, slot):
        off = pl.multiple_of(i * tk, tk)
        m_prev = m_ref[...]
        m_new = jnp.maximum(m_prev, mx_ref[slot])
        alpha = jnp.exp2(m_prev - m_new)
        p = jnp.exp2(s_ref[slot] - m_new).astype(bf16)
        pv = jnp.dot(vt_ref[0, :, pl.ds(off, tk)], p, preferred_element_type=f32)
        acc_ref[...] = alpha * acc_ref[...] + pv
        m_ref[...] = m_new

    scores(0, 0)

    s = jnp.dot(km_ref[0], qt, preferred_element_type=f32)
    row = lax.broadcasted_iota(jnp.int32, s.shape, 0)
    s = jnp.where(row < N_META, s, NEG)
    m0 = jnp.max(s, axis=0, keepdims=True)
    p = jnp.exp2(s - m0).astype(bf16)
    acc_ref[...] = jnp.dot(vtm_ref[0], p, preferred_element_type=f32)
    m_ref[...] = m0

    def body(j, carry):
        scores(2 * j + 1, 1)
        accumulate(2 * j, 0)
        scores(2 * j + 2, 0)
        accumulate(2 * j + 1, 1)
        return carry

    lax.fori_loop(0, nk // 2 - 1, body, 0)
    scores(nk - 1, 1)
    accumulate(nk - 2, 0)
    accumulate(nk - 1, 1)
    l_ref[...] = acc_ref[MLA_V:MLA_V + 1]


def _mla_fixed_offset(c, qt, k_ref, vt_ref, km_ref, vtm_ref, acc_ref, l_ref, tk):
    nk = k_ref.shape[1] // tk
    assert nk % MLA_TILES_PER_TRIP == 0

    sub = 8

    def col_sums(p):
        return jnp.sum(p.reshape(p.shape[0] // sub, sub, p.shape[1]), axis=0)

    s = jnp.dot(km_ref[0], qt, preferred_element_type=f32)
    row = lax.broadcasted_iota(jnp.int32, s.shape, 0)
    p = jnp.exp2(jnp.where(row < N_META, s - c, NEG))
    acc_ref[0:MLA_V] = jnp.dot(vtm_ref[0, 0:MLA_V], p.astype(bf16), preferred_element_type=f32)
    acc_ref[MLA_V:MLA_V + sub] = col_sums(p)

    def tile(i):
        off = pl.multiple_of(i * tk, tk)
        for qb in range(qt.shape[1] // MXU_DIM):
            cols = slice(qb * MXU_DIM, (qb + 1) * MXU_DIM)
            s = jnp.dot(k_ref[0, pl.ds(off, tk), :], qt[:, cols], preferred_element_type=f32)
            p = jnp.exp2(s - c)
            acc_ref[0:MLA_V, cols] += jnp.dot(vt_ref[0, 0:MLA_V, pl.ds(off, tk)], p.astype(bf16),
                                              preferred_element_type=f32)
            acc_ref[MLA_V:MLA_V + sub, cols] += col_sums(p)

    def body(j, carry):
        for u in range(MLA_TILES_PER_TRIP):
            tile(j * MLA_TILES_PER_TRIP + u)
        return carry

    lax.fori_loop(0, nk // MLA_TILES_PER_TRIP, body, 0)
    l_ref[...] = jnp.sum(acc_ref[MLA_V:MLA_V + sub], axis=0, keepdims=True)


def _mla_kernel(c_ref, qt_ref, k_ref, km_ref, qt8_ref, k8_ref, km8_ref, vt_ref, vtm_ref, g_ref, o_ref,
                acc_ref, l_ref, m_ref, mx_ref, s_ref, *, tk):
    c = c_ref[0]
    bounded = c <= MLA_FIXED_OFFSET_MAX

    @pl.when(bounded)
    def _():
        _mla_fixed_offset(c, qt8_ref[0], k8_ref, vt_ref, km8_ref, vtm_ref, acc_ref, l_ref, tk)

    @pl.when(jnp.logical_not(bounded))
    def _():
        _mla_online(qt_ref[0], k_ref, vt_ref, km_ref, vtm_ref, acc_ref, l_ref, m_ref, mx_ref, s_ref, tk)

    o = acc_ref[0:MLA_V] / l_ref[...]
    g = g_ref[...].astype(f32)
    o_ref[...] = (o * _silu(g)).astype(bf16)


def _mla_score_bound(qn_w, qpe_w, kn_w, kpe_w):
    def sq(nope_w, pe_w):
        return MLA_NOPE * jnp.max(nope_w.astype(f32) ** 2) + MLA_ROPE * jnp.max(pe_w.astype(f32) ** 2)
    margin = (1.0 + 2.0 ** -4) ** 2 * 1.01
    slack = 0.5
    return (margin * MLA_QK_SCALE ** 2 * jnp.sqrt(sq(qn_w, qpe_w) * sq(kn_w, kpe_w)) + slack).reshape(1)


def _mla(c, qt, k, km, qt8, k8, km8, vt, vtm, gmt, tq, tk):
    n = qt.shape[2]
    grid = (MLA_HEADS, n // tq)
    return pl.pallas_call(
        functools.partial(_mla_kernel, tk=tk),
        grid=grid,
        in_specs=[
            pl.BlockSpec(memory_space=pltpu.SMEM),
            pl.BlockSpec((1, QK_DIM, tq), lambda h, j: (h, 0, j)),
            pl.BlockSpec((1, n, QK_DIM), lambda h, j: (h, 0, 0)),
            pl.BlockSpec((1, META_PAD, QK_DIM), lambda h, j: (h, 0, 0)),
            pl.BlockSpec((1, QK_DIM, tq), lambda h, j: (h, 0, j)),
            pl.BlockSpec((1, n, QK_DIM), lambda h, j: (h, 0, 0)),
            pl.BlockSpec((1, META_PAD, QK_DIM), lambda h, j: (h, 0, 0)),
            pl.BlockSpec((1, V_ROWS, n), lambda h, j: (h, 0, 0)),
            pl.BlockSpec((1, V_ROWS, META_PAD), lambda h, j: (h, 0, 0)),
            pl.BlockSpec((MLA_V, tq), lambda h, j: (h, j)),
        ],
        out_specs=pl.BlockSpec((MLA_V, tq), lambda h, j: (h, j)),
        out_shape=jax.ShapeDtypeStruct((MLA_WIDTH, n), bf16),
        scratch_shapes=[pltpu.VMEM((V_ROWS, tq), f32), pltpu.VMEM((1, tq), f32), pltpu.VMEM((1, tq), f32),
                        pltpu.VMEM((2, 1, tq), f32), pltpu.VMEM((2, tk, tq), f32)],
        compiler_params=pltpu.CompilerParams(
            dimension_semantics=("arbitrary", "arbitrary"), vmem_limit_bytes=VMEM_LIMIT_BYTES),
        name="mla",
    )(c, qt, k, km, qt8, k8, km8, vt, vtm, gmt)


NA_DR = 2 * NA_WIN_R - 1
NA_DC = 2 * NA_WIN_C - 1


def _na_pair_plan():
    r = np.arange(ROWS)
    r0 = np.clip(r - NA_WIN_R // 2, 0, ROWS - NA_WIN_R)
    r0e = np.minimum(r0 - (r0 % 2), ROWS - NA_SLAB_ROWS)[0::2]
    r0e_row = np.repeat(r0e, 2)
    kr = r0e_row[:, None] + np.arange(NA_SLAB_ROWS)[None, :]
    in_win = (kr >= r0[:, None]) & (kr < r0[:, None] + NA_WIN_R)
    assert (in_win.sum(axis=1) == NA_WIN_R).all()
    dr = kr - r[:, None] + (NA_WIN_R - 1)
    blk = np.where(in_win, dr, NA_DR)
    assert blk.min() >= 0 and blk.max() <= NA_DR
    return r0e.astype(np.int32), blk.reshape(-1).astype(np.int32)


def _na_score_bounds(q_w, k_w, rel_bias, meta_bias):
    qk = 1.02 * NA_HEAD_DIM ** -0.5 * LOG2E * NA_HEAD_DIM * jnp.max(jnp.abs(q_w.astype(f32))) * jnp.max(
        jnp.abs(k_w.astype(f32)))
    b_hi = LOG2E * jnp.maximum(jnp.max(rel_bias), jnp.max(meta_bias)).astype(f32)
    b_lo = LOG2E * jnp.minimum(jnp.min(rel_bias), jnp.min(meta_bias)).astype(f32)
    bounded = 2.0 * qk + (b_hi - b_lo) <= NA_FIXED_OFFSET_SPAN
    return qk + b_hi, bounded


def _na_tables(rel_bias, meta_bias, shift):
    c = np.arange(GRID_W)
    c0 = np.clip(c - NA_WIN_C // 2, 0, GRID_W - NA_WIN_C)
    kc = np.arange(GRID_W)
    valid_c = (kc[:, None] >= c0[None, :]) & (kc[:, None] < c0[None, :] + NA_WIN_C)
    dc = kc[:, None] - c[None, :] + (NA_WIN_C - 1)
    sel = (np.arange(NA_DC)[:, None, None] == dc[None]) & valid_c[None]
    toe = jnp.einsum("hrd,dkc->hrkc", rel_bias.astype(f32), jnp.asarray(sel, f32),
                     precision=lax.Precision.HIGHEST)
    t = jnp.where(valid_c[None, None], toe * LOG2E - shift, NEG)
    t = jnp.concatenate([t, jnp.full((NA_HEADS, 1, GRID_W, GRID_W), NEG, f32)], axis=1)
    ng = NA_HEADS // NA_GROUP
    t = t.reshape(ng, NA_GROUP, NA_DR + 1, GRID_W, GRID_W)
    t = t.transpose(0, 2, 3, 1, 4).reshape(ng, NA_DR + 1, GRID_W, NA_GROUP * GRID_W)
    mb = meta_bias.astype(f32) * LOG2E - shift
    mb = jnp.concatenate([mb, jnp.full((NA_HEADS, META_PAD - N_META), NEG, f32)], axis=1)
    tm = jnp.broadcast_to(mb.reshape(ng, NA_GROUP, META_PAD, 1), (ng, NA_GROUP, META_PAD, GRID_W))
    tm = tm.transpose(0, 2, 1, 3).reshape(ng, META_PAD, NA_GROUP * GRID_W)
    return t, tm


def _na_kernel(r0e_ref, blk_ref, flag_ref, q_ref, k_ref, vt_ref, km_ref, vtm_ref, t_ref, tm_ref, g_ref,
               wmask_ref, o_ref, s_ref):
    b = pl.program_id(1)
    gw = NA_GROUP * GRID_W
    lane_lo = lax.broadcasted_iota(jnp.int32, (gw, LANES), 1) < GRID_W
    head_even = lax.broadcasted_iota(jnp.int32, (gw, LANES), 0) < NA_HEAD_DIM
    pairs_per_step = NA_ROWS_PER_STEP // 2

    def scores(pi, slot):
        pg = b * pairs_per_step + pi
        rows = pl.ds(pl.multiple_of(pi * 2 * GRID_W, 2 * GRID_W), 2 * GRID_W)
        q2 = q_ref[rows, :]
        qa, qb = q2[:GRID_W], q2[GRID_W:]
        wt = jnp.concatenate([qa] * NA_GROUP + [qb] * NA_GROUP, axis=0) * wmask_ref[...]
        koff = pl.multiple_of(r0e_ref[pg] * GRID_W, LANES)
        keys = jnp.concatenate([k_ref[pl.ds(koff, NA_SLAB), :], km_ref[...]], axis=0)

        def table(rr):
            base = (pg * 2 + rr) * NA_SLAB_ROWS
            return jnp.concatenate([t_ref[0, blk_ref[base + j]] for j in range(NA_SLAB_ROWS)] + [tm_ref[0]], axis=0)

        bias = jnp.concatenate([table(0), table(1)], axis=1)
        s_ref[slot] = lax.dot_general(keys, wt, _NT, preferred_element_type=f32) + bias

    def finish(pi, slot, use_max):
        pg = b * pairs_per_step + pi
        koff = pl.multiple_of(r0e_ref[pg] * GRID_W, LANES)
        s = s_ref[slot]
        if use_max:
            s = s - jnp.max(s, axis=0, keepdims=True)
        p = jnp.exp2(s)
        l = jnp.sum(p, axis=0, keepdims=True)
        vals = jnp.concatenate([vt_ref[:, pl.ds(koff, NA_SLAB)], vtm_ref[...]], axis=1)
        out = jnp.dot(vals, p.astype(bf16), preferred_element_type=f32) * (1.0 / l)
        oa, ob = out[:, :LANES], out[:, LANES:]
        ya = jnp.where(head_even, oa, pltpu.roll(oa, GRID_W, axis=1))
        yb = jnp.where(head_even, pltpu.roll(ob, GRID_W, axis=1), ob)
        y = jnp.where(lane_lo, ya, yb)
        loff = pl.multiple_of(pi * LANES, LANES)
        g = g_ref[:, pl.ds(loff, LANES)].astype(f32)
        o_ref[:, pl.ds(loff, LANES)] = (y * _silu(g)).astype(bf16)

    def sweep(use_max):
        assert NA_PAIRS_PER_TRIP % 2 == 0
        scores(0, 0)

        def trip(t, carry):
            for u in range(NA_PAIRS_PER_TRIP):
                pi = t * NA_PAIRS_PER_TRIP + u
                scores(jnp.minimum(pi + 1, pairs_per_step - 1), (u + 1) % 2)
                finish(pi, u % 2, use_max)
            return carry
        lax.fori_loop(0, pairs_per_step // NA_PAIRS_PER_TRIP, trip, 0)

    bounded = flag_ref[0] != 0

    @pl.when(bounded)
    def _():
        sweep(False)

    @pl.when(jnp.logical_not(bounded))
    def _():
        sweep(True)


def _na(naq, nak, navt, nakm, navtm, t, tmeta, gnt, bounded):
    r0e, blk = _na_pair_plan()
    ng = NA_HEADS // NA_GROUP
    gw = NA_GROUP * GRID_W
    assert gw == LANES
    wmask = jnp.asarray(np.tile(np.kron(np.eye(NA_GROUP), np.ones((GRID_W, NA_HEAD_DIM))), (2, 1)), bf16)
    tq = NA_ROWS_PER_STEP * GRID_W
    grid = (ng, ROWS // NA_ROWS_PER_STEP)
    grid_spec = pltpu.PrefetchScalarGridSpec(
        num_scalar_prefetch=3,
        grid=grid,
        in_specs=[
            pl.BlockSpec((tq, gw), lambda g, b, *_: (b, g)),
            pl.BlockSpec((SEQ, gw), lambda g, b, *_: (0, g)),
            pl.BlockSpec((gw, SEQ), lambda g, b, *_: (g, 0)),
            pl.BlockSpec((META_PAD, gw), lambda g, b, *_: (0, g)),
            pl.BlockSpec((gw, META_PAD), lambda g, b, *_: (g, 0)),
            pl.BlockSpec((1, NA_DR + 1, GRID_W, gw), lambda g, b, *_: (g, 0, 0, 0)),
            pl.BlockSpec((1, META_PAD, gw), lambda g, b, *_: (g, 0, 0)),
            pl.BlockSpec((gw, tq), lambda g, b, *_: (g, b)),
            pl.BlockSpec((2 * gw, gw), lambda g, b, *_: (0, 0)),
        ],
        out_specs=pl.BlockSpec((gw, tq), lambda g, b, *_: (g, b)),
        scratch_shapes=[pltpu.VMEM((2, NA_SLAB + META_PAD, 2 * gw), f32)],
    )
    return pl.pallas_call(
        _na_kernel,
        grid_spec=grid_spec,
        out_shape=jax.ShapeDtypeStruct((NA_WIDTH, SEQ), bf16),
        compiler_params=pltpu.CompilerParams(
            dimension_semantics=("arbitrary", "arbitrary"), vmem_limit_bytes=VMEM_LIMIT_BYTES),
        name="na",
    )(jnp.asarray(r0e), jnp.asarray(blk), bounded.astype(jnp.int32).reshape(1),
      naq, nak, navt, nakm, navtm, t, tmeta, gnt, wmask)


def _out_kernel(x_ref, ym_ref, yn_ref, wm_ref, wn_ref, o_ref):
    acc = lax.dot_general(ym_ref[...], wm_ref[...], _TN, preferred_element_type=f32)
    acc = acc + lax.dot_general(yn_ref[...], wn_ref[...], _TN, preferred_element_type=f32)
    o_ref[...] = x_ref[...] + acc


def _out_proj(x, ymt, ynt, wm, wn, tm):
    n = x.shape[0]
    return pl.pallas_call(
        _out_kernel,
        grid=(n // tm,),
        in_specs=[
            pl.BlockSpec((tm, D_MODEL), lambda i: (i, 0)),
            pl.BlockSpec((MLA_WIDTH, tm), lambda i: (0, i)),
            pl.BlockSpec((NA_WIDTH, tm), lambda i: (0, i)),
            _const_spec(wm.shape),
            _const_spec(wn.shape),
        ],
        out_specs=pl.BlockSpec((tm, D_MODEL), lambda i: (i, 0)),
        out_shape=jax.ShapeDtypeStruct((n, D_MODEL), f32),
        compiler_params=pltpu.CompilerParams(
            dimension_semantics=("arbitrary",), vmem_limit_bytes=VMEM_LIMIT_BYTES),
        name="out_proj",
    )(x, ymt, ynt, wm, wn)


def _prep_weights(norm_w, w_in, q_lat_norm_w, kv_lat_norm_w, w_uq, w_ukv,
                  mla_qn_w, mla_qpe_w, mla_kn_w, mla_kpe_w, na_q_norm_w, na_k_norm_w):
    sizes = (MLA_Q_RANK, MLA_KV_RANK, MLA_ROPE, MLA_WIDTH, NA_WIDTH, NA_WIDTH, NA_WIDTH, NA_WIDTH)
    o = np.concatenate([[0], np.cumsum(sizes)])
    w_in = w_in.astype(bf16)
    seg = [w_in[:, o[i]:o[i + 1]] for i in range(8)]
    wa = jnp.concatenate([seg[0], seg[1], seg[4], seg[5]], axis=1)
    wbt = jnp.concatenate([seg[3], seg[6], seg[7], seg[2]], axis=1).T
    ukv = w_ukv.reshape(MLA_KV_RANK, MLA_HEADS, 2, MLA_NOPE)
    return {
        "norm_w": norm_w.reshape(1, D_MODEL).astype(f32),
        "wa": wa,
        "wbt": wbt,
        "qlw": q_lat_norm_w.reshape(1, MLA_Q_RANK).astype(f32),
        "kvlw": kv_lat_norm_w.reshape(1, MLA_KV_RANK).astype(f32),
        "wuqt": w_uq.T.astype(bf16),
        "wuk": ukv[:, :, 0, :].reshape(MLA_KV_RANK, MLA_HEADS * MLA_NOPE).astype(bf16),
        "wuvt": ukv[:, :, 1, :].reshape(MLA_KV_RANK, MLA_HEADS * MLA_V).T.astype(bf16),
        "qnw": mla_qn_w.reshape(MLA_NOPE, 1).astype(f32),
        "qpew": mla_qpe_w.reshape(MLA_ROPE, 1).astype(f32),
        "knw": mla_kn_w.reshape(1, MLA_NOPE).astype(f32),
        "kpew": mla_kpe_w.reshape(MLA_ROPE, 1).astype(f32),
        "naqw": jnp.tile(na_q_norm_w.reshape(1, NA_HEAD_DIM), (1, 2)).astype(f32),
        "nakw": jnp.tile(na_k_norm_w.reshape(1, NA_HEAD_DIM), (1, 2)).astype(f32),
    }


def kernel(x, meta_tokens, norm_w, w_in, q_lat_norm_w, kv_lat_norm_w, w_uq, w_ukv,
           mla_qn_w, mla_qpe_w, mla_kn_w, mla_kpe_w, na_q_norm_w, na_k_norm_w,
           na_rel_bias, na_meta_bias, w_out):
    assert x.shape == (1, SEQ, D_MODEL) and norm_w.shape[0] == 1
    xr = x[0]
    w = _prep_weights(norm_w[0], w_in[0], q_lat_norm_w[0], kv_lat_norm_w[0], w_uq[0], w_ukv[0],
                      mla_qn_w[0], mla_qpe_w[0], mla_kn_w[0], mla_kpe_w[0],
                      na_q_norm_w[0], na_k_norm_w[0])

    pos = jnp.arange(N_META + SEQ, dtype=f32)
    inv_freq = ROPE_BASE ** (-(jnp.arange(0, MLA_ROPE, 2, dtype=f32) / MLA_ROPE))
    ang = pos[:, None] * inv_freq[None, :]
    cos_t, sin_t = jnp.cos(ang).T, jnp.sin(ang).T
    pad = ((0, 0), (0, META_PAD - N_META))
    cos_m, sin_m = jnp.pad(cos_t[:, :N_META], pad), jnp.pad(sin_t[:, :N_META], pad)

    xm = jnp.pad(meta_tokens.astype(f32), ((0, META_PAD - N_META), (0, 0)))

    qt, k, qt8, k8, vt, gmt, naq, nak, navt, gnt = _project(xr, cos_t[:, N_META:], sin_t[:, N_META:], w, tm=256)
    _, km, _, km8, vtm, _, _, nakm, navtm, _ = _project(xm, cos_m, sin_m, w, tm=META_PAD)

    c = _mla_score_bound(mla_qn_w[0], mla_qpe_w[0], mla_kn_w[0], mla_kpe_w[0])
    ymt = _mla(c, qt, k, km, qt8, k8, km8, vt, vtm, gmt, tq=1024, tk=1024)

    na_off, na_bounded = _na_score_bounds(na_q_norm_w[0], na_k_norm_w[0], na_rel_bias[0], na_meta_bias[0])
    t, tmeta = _na_tables(na_rel_bias[0], na_meta_bias[0], jnp.where(na_bounded, na_off, 0.0))
    ynt = _na(naq, nak, navt, nakm, navtm, t, tmeta, gnt, na_bounded)

    wo = w_out[0].astype(bf16)
    out = _out_proj(xr, ymt, ynt, wo[:MLA_WIDTH], wo[MLA_WIDTH:], tm=1024)
    return out[None]
```

```python
import functools
import math

import numpy as np
import jax
import jax.numpy as jnp
from jax import lax
from jax.experimental import pallas as pl
from jax.experimental.pallas import tpu as pltpu

D_MODEL = 1024
SEQ = 16384
N_META = 16
GRID_W = 64
ROWS = SEQ // GRID_W
EPS = 1e-6

MLA_HEADS = 8
MLA_NOPE = 128
MLA_ROPE = 64
MLA_V = 128
MLA_Q_RANK = 256
MLA_KV_RANK = 256
MLA_WIDTH = MLA_HEADS * MLA_V
ROPE_BASE = 10000.0

NA_HEADS = 16
NA_HEAD_DIM = 64
NA_WIDTH = NA_HEADS * NA_HEAD_DIM
NA_WIN_R = 8
NA_WIN_C = 16

LOG2E = math.log2(math.e)
NEG = -1e30

LANES = 128
MXU_DIM = 256
VMEM_LIMIT_BYTES = 56 * 1024 * 1024

META_PAD = 128
QK_DIM = MXU_DIM
V_ROWS = MLA_V + 16

NA_GROUP = 2
NA_SLAB_ROWS = 10
NA_SLAB = NA_SLAB_ROWS * GRID_W
NA_ROWS_PER_STEP = 128
NA_PAIRS_PER_TRIP = 16

_NT = (((1,), (1,)), ((), ()))
_TN = (((0,), (0,)), ((), ()))

f32 = jnp.float32
bf16 = jnp.bfloat16

MLA_QK_FP8 = jnp.float8_e4m3fn
MLA_QK_SCALE = math.sqrt((MLA_NOPE + MLA_ROPE) ** -0.5 * LOG2E)
MLA_FIXED_OFFSET_MAX = 60.0
MLA_TILES_PER_TRIP = 8
NA_FIXED_OFFSET_SPAN = 120.0


def _rms(x, axis):
    return lax.rsqrt(jnp.mean(x * x, axis=axis, keepdims=True) + EPS)


def _silu(g):
    return g / (1.0 + jnp.exp(-g))


def _proj_kernel(x_ref, cos_ref, sin_ref, nw_ref, wa_ref, wbt_ref, qlw_ref, kvlw_ref,
                 wuqt_ref, wuk_ref, wuvt_ref, qnw_ref, qpew_ref, knw_ref, kpew_ref,
                 naqw_ref, nakw_ref,
                 qt_ref, k_ref, qt8_ref, k8_ref, vt_ref, gmt_ref, naq_ref, nak_ref, navt_ref, gnt_ref):
    tm = x_ref.shape[0]
    x = x_ref[...]
    h = (x * _rms(x, 1) * nw_ref[...]).astype(bf16)
    pa = jnp.dot(h, wa_ref[...], preferred_element_type=f32)
    pbt = lax.dot_general(wbt_ref[...], h, _NT, preferred_element_type=f32)

    gmt_ref[...] = pbt[0:MLA_WIDTH].astype(bf16)
    navt_ref[...] = pbt[MLA_WIDTH:MLA_WIDTH + NA_WIDTH].astype(bf16)
    gnt_ref[...] = pbt[MLA_WIDTH + NA_WIDTH:MLA_WIDTH + 2 * NA_WIDTH].astype(bf16)

    cos = cos_ref[...]
    sin = sin_ref[...]
    half = MLA_ROPE // 2

    kpe = pbt[MLA_WIDTH + 2 * NA_WIDTH:MLA_WIDTH + 2 * NA_WIDTH + MLA_ROPE]
    kpe = kpe * _rms(kpe, 0) * kpew_ref[...]
    k1, k2 = kpe[:half], kpe[half:]
    kro = jnp.concatenate(
        [k1 * cos - k2 * sin, k2 * cos + k1 * sin, jnp.zeros((LANES - MLA_ROPE, tm), f32)], axis=0)
    kpe_nat = kro.T * MLA_QK_SCALE

    qscale = MLA_QK_SCALE
    ql = pa[:, 0:MLA_Q_RANK]
    qln = (ql * _rms(ql, 1) * qlw_ref[...]).astype(bf16)
    qt = lax.dot_general(wuqt_ref[...], qln, _NT, preferred_element_type=f32)
    per_head = MLA_NOPE + MLA_ROPE
    for hh in range(MLA_HEADS):
        base = per_head * hh
        qn = qt[base:base + MLA_NOPE]
        qn = qn * _rms(qn, 0) * qnw_ref[...]
        qp = qt[base + MLA_NOPE:base + per_head]
        qp = qp * _rms(qp, 0) * qpew_ref[...]
        q1, q2 = qp[:half], qp[half:]
        qh = jnp.concatenate([qn, q1 * cos - q2 * sin, q2 * cos + q1 * sin], axis=0) * qscale
        for dst in (qt_ref, qt8_ref):
            dst[hh, 0:per_head, :] = qh.astype(dst.dtype)
            dst[hh, per_head:QK_DIM, :] = jnp.zeros((QK_DIM - per_head, tm), dst.dtype)

    kvl = pa[:, MLA_Q_RANK:MLA_Q_RANK + MLA_KV_RANK]
    kvn = (kvl * _rms(kvl, 1) * kvlw_ref[...]).astype(bf16)
    kn = jnp.dot(kvn, wuk_ref[...], preferred_element_type=f32)
    vt = lax.dot_general(wuvt_ref[...], kvn, _NT, preferred_element_type=f32)
    for hh in range(MLA_HEADS):
        xh = kn[:, MLA_NOPE * hh:MLA_NOPE * (hh + 1)]
        xh = xh * _rms(xh, 1) * knw_ref[...]
        for dst in (k_ref, k8_ref):
            dst[hh, :, 0:MLA_NOPE] = (xh * MLA_QK_SCALE).astype(dst.dtype)
            dst[hh, :, MLA_NOPE:QK_DIM] = kpe_nat.astype(dst.dtype)
        vt_ref[hh, 0:MLA_V, :] = vt[MLA_V * hh:MLA_V * (hh + 1)].astype(bf16)
        vt_ref[hh, MLA_V:V_ROWS, :] = jnp.ones((V_ROWS - MLA_V, tm), bf16)

    lane = lax.broadcasted_iota(jnp.int32, (tm, LANES), 1)
    lo = lane < NA_HEAD_DIM
    na_scale = NA_HEAD_DIM ** -0.5 * LOG2E
    q_off = MLA_Q_RANK + MLA_KV_RANK
    k_off = q_off + NA_WIDTH
    for p in range(NA_WIDTH // LANES):
        for off, w_ref, dst, scale in ((q_off, naqw_ref, naq_ref, na_scale), (k_off, nakw_ref, nak_ref, 1.0)):
            xp = pa[:, off + LANES * p:off + LANES * (p + 1)]
            sq = xp * xp
            s_lo = jnp.sum(jnp.where(lo, sq, 0.0), axis=1, keepdims=True)
            s_hi = jnp.sum(jnp.where(lo, 0.0, sq), axis=1, keepdims=True)
            r = jnp.where(lo, lax.rsqrt(s_lo / NA_HEAD_DIM + EPS), lax.rsqrt(s_hi / NA_HEAD_DIM + EPS))
            y = xp * r * w_ref[...]
            if scale != 1.0:
                y = y * scale
            dst[:, LANES * p:LANES * (p + 1)] = y.astype(bf16)


def _const_spec(shape):
    nd = len(shape)
    return pl.BlockSpec(shape, lambda i, _nd=nd: (0,) * _nd, pipeline_mode=pl.Buffered(1))


def _project(x, cos_t, sin_t, w, tm):
    n = x.shape[1]
    grid = (n // tm,)
    weights = (w["norm_w"], w["wa"], w["wbt"], w["qlw"], w["kvlw"], w["wuqt"], w["wuk"], w["wuvt"],
               w["qnw"], w["qpew"], w["knw"], w["kpew"], w["naqw"], w["nakw"])
    in_specs = [pl.BlockSpec((None, tm, D_MODEL), lambda i: (0, i, 0)),
                pl.BlockSpec((MLA_ROPE // 2, tm), lambda i: (0, i)),
                pl.BlockSpec((MLA_ROPE // 2, tm), lambda i: (0, i))]
    in_specs += [_const_spec(a.shape) for a in weights]
    out_shape = (
        jax.ShapeDtypeStruct((MLA_HEADS, QK_DIM, n), bf16),
        jax.ShapeDtypeStruct((MLA_HEADS, n, QK_DIM), bf16),
        jax.ShapeDtypeStruct((MLA_HEADS, QK_DIM, n), MLA_QK_FP8),
        jax.ShapeDtypeStruct((MLA_HEADS, n, QK_DIM), MLA_QK_FP8),
        jax.ShapeDtypeStruct((MLA_HEADS, V_ROWS, n), bf16),
        jax.ShapeDtypeStruct((MLA_WIDTH, n), bf16),
        jax.ShapeDtypeStruct((n, NA_WIDTH), bf16),
        jax.ShapeDtypeStruct((n, NA_WIDTH), bf16),
        jax.ShapeDtypeStruct((NA_WIDTH, n), bf16),
        jax.ShapeDtypeStruct((NA_WIDTH, n), bf16),
    )
    out_specs = (
        pl.BlockSpec((MLA_HEADS, QK_DIM, tm), lambda i: (0, 0, i)),
        pl.BlockSpec((MLA_HEADS, tm, QK_DIM), lambda i: (0, i, 0)),
        pl.BlockSpec((MLA_HEADS, QK_DIM, tm), lambda i: (0, 0, i)),
        pl.BlockSpec((MLA_HEADS, tm, QK_DIM), lambda i: (0, i, 0)),
        pl.BlockSpec((MLA_HEADS, V_ROWS, tm), lambda i: (0, 0, i)),
        pl.BlockSpec((MLA_WIDTH, tm), lambda i: (0, i)),
        pl.BlockSpec((tm, NA_WIDTH), lambda i: (i, 0)),
        pl.BlockSpec((tm, NA_WIDTH), lambda i: (i, 0)),
        pl.BlockSpec((NA_WIDTH, tm), lambda i: (0, i)),
        pl.BlockSpec((NA_WIDTH, tm), lambda i: (0, i)),
    )
    return pl.pallas_call(
        _proj_kernel,
        grid=grid,
        in_specs=in_specs,
        out_specs=out_specs,
        out_shape=out_shape,
        compiler_params=pltpu.CompilerParams(
            dimension_semantics=("arbitrary",), vmem_limit_bytes=VMEM_LIMIT_BYTES),
        name="proj",
    )(x, cos_t, sin_t, *weights)


def _mla_online(qt, k_ref, vt_ref, km_ref, vtm_ref, acc_ref, l_ref, m_ref, mx_ref, s_ref, tk):
    nk = k_ref.shape[1] // tk
    assert nk % 2 == 0 and nk >= 2

    def scores(i, slot):
        off = pl.multiple_of(i * tk, tk)
        s = jnp.dot(k_ref[0, pl.ds(off, tk), :], qt, preferred_element_type=f32)
        s_ref[slot] = s
        mx_ref[slot] = jnp.max(s, axis=0, keepdims=True)

    def accumulate(i, slot):
        off = pl.multiple_of(i * tk, tk)
        m_prev = m_ref[...]
        m_new = jnp.maximum(m_prev, mx_ref[slot])
        alpha = jnp.exp2(m_prev - m_new)
        p = jnp.exp2(s_ref[slot] - m_new).astype(bf16)
        pv = jnp.dot(vt_ref[0, :, pl.ds(off, tk)], p, preferred_element_type=f32)
        acc_ref[...] = alpha * acc_ref[...] + pv
        m_ref[...] = m_new

    scores(0, 0)

    s = jnp.dot(km_ref[0], qt, preferred_element_type=f32)
    row = lax.broadcasted_iota(jnp.int32, s.shape, 0)
    s = jnp.where(row < N_META, s, NEG)
    m0 = jnp.max(s, axis=0, keepdims=True)
    p = jnp.exp2(s - m0).astype(bf16)
    acc_ref[...] = jnp.dot(vtm_ref[0], p, preferred_element_type=f32)
    m_ref[...] = m0

    def body(j, carry):
        scores(2 * j + 1, 1)
        accumulate(2 * j, 0)
        scores(2 * j + 2, 0)
        accumulate(2 * j + 1, 1)
        return carry

    lax.fori_loop(0, nk // 2 - 1, body, 0)
    scores(nk - 1, 1)
    accumulate(nk - 2, 0)
    accumulate(nk - 1, 1)
    l_ref[...] = acc_ref[MLA_V:MLA_V + 1]


def _mla_fixed_offset(c, qt, k_ref, vt_ref, km_ref, vtm_ref, acc_ref, l_ref, tk):
    nk = k_ref.shape[1] // tk
    assert nk % MLA_TILES_PER_TRIP == 0

    sub = 8

    def col_sums(p):
        return jnp.sum(p.reshape(p.shape[0] // sub, sub, p.shape[1]), axis=0)

    s = jnp.dot(km_ref[0], qt, preferred_element_type=f32)
    row = lax.broadcasted_iota(jnp.int32, s.shape, 0)
    p = jnp.exp2(jnp.where(row < N_META, s - c, NEG))
    acc_ref[0:MLA_V] = jnp.dot(vtm_ref[0, 0:MLA_V], p.astype(bf16), preferred_element_type=f32)
    acc_ref[MLA_V:MLA_V + sub] = col_sums(p)

    def tile(i):
        off = pl.multiple_of(i * tk, tk)
        for qb in range(qt.shape[1] // MXU_DIM):
            cols = slice(qb * MXU_DIM, (qb + 1) * MXU_DIM)
            s = jnp.dot(k_ref[0, pl.ds(off, tk), :], qt[:, cols], preferred_element_type=f32)
            p = jnp.exp2(s - c)
            acc_ref[0:MLA_V, cols] += jnp.dot(vt_ref[0, 0:MLA_V, pl.ds(off, tk)], p.astype(bf16),
                                              preferred_element_type=f32)
            acc_ref[MLA_V:MLA_V + sub, cols] += col_sums(p)

    def body(j, carry):
        for u in range(MLA_TILES_PER_TRIP):
            tile(j * MLA_TILES_PER_TRIP + u)
        return carry

    lax.fori_loop(0, nk // MLA_TILES_PER_TRIP, body, 0)
    l_ref[...] = jnp.sum(acc_ref[MLA_V:MLA_V + sub], axis=0, keepdims=True)


def _mla_kernel(c_ref, qt_ref, k_ref, km_ref, qt8_ref, k8_ref, km8_ref, vt_ref, vtm_ref, g_ref, o_ref,
                acc_ref, l_ref, m_ref, mx_ref, s_ref, *, tk):
    c = c_ref[0]
    bounded = c <= MLA_FIXED_OFFSET_MAX

    @pl.when(bounded)
    def _():
        _mla_fixed_offset(c, qt8_ref[0], k8_ref, vt_ref, km8_ref, vtm_ref, acc_ref, l_ref, tk)

    @pl.when(jnp.logical_not(bounded))
    def _():
        _mla_online(qt_ref[0], k_ref, vt_ref, km_ref, vtm_ref, acc_ref, l_ref, m_ref, mx_ref, s_ref, tk)

    o = acc_ref[0:MLA_V] / l_ref[...]
    g = g_ref[...].astype(f32)
    o_ref[...] = (o * _silu(g)).astype(bf16)


def _mla_score_bound(qn_w, qpe_w, kn_w, kpe_w):
    def sq(nope_w, pe_w):
        return MLA_NOPE * jnp.max(nope_w.astype(f32) ** 2) + MLA_ROPE * jnp.max(pe_w.astype(f32) ** 2)
    margin = (1.0 + 2.0 ** -4) ** 2 * 1.01
    slack = 0.5
    return (margin * MLA_QK_SCALE ** 2 * jnp.sqrt(sq(qn_w, qpe_w) * sq(kn_w, kpe_w)) + slack).reshape(1)


def _mla(c, qt, k, km, qt8, k8, km8, vt, vtm, gmt, tq, tk):
    n = qt.shape[2]
    grid = (MLA_HEADS, n // tq)
    return pl.pallas_call(
        functools.partial(_mla_kernel, tk=tk),
        grid=grid,
        in_specs=[
            pl.BlockSpec(memory_space=pltpu.SMEM),
            pl.BlockSpec((1, QK_DIM, tq), lambda h, j: (h, 0, j)),
            pl.BlockSpec((1, n, QK_DIM), lambda h, j: (h, 0, 0)),
            pl.BlockSpec((1, META_PAD, QK_DIM), lambda h, j: (h, 0, 0)),
            pl.BlockSpec((1, QK_DIM, tq), lambda h, j: (h, 0, j)),
            pl.BlockSpec((1, n, QK_DIM), lambda h, j: (h, 0, 0)),
            pl.BlockSpec((1, META_PAD, QK_DIM), lambda h, j: (h, 0, 0)),
            pl.BlockSpec((1, V_ROWS, n), lambda h, j: (h, 0, 0)),
            pl.BlockSpec((1, V_ROWS, META_PAD), lambda h, j: (h, 0, 0)),
            pl.BlockSpec((MLA_V, tq), lambda h, j: (h, j)),
        ],
        out_specs=pl.BlockSpec((MLA_V, tq), lambda h, j: (h, j)),
        out_shape=jax.ShapeDtypeStruct((MLA_WIDTH, n), bf16),
        scratch_shapes=[pltpu.VMEM((V_ROWS, tq), f32), pltpu.VMEM((1, tq), f32), pltpu.VMEM((1, tq), f32),
                        pltpu.VMEM((2, 1, tq), f32), pltpu.VMEM((2, tk, tq), f32)],
        compiler_params=pltpu.CompilerParams(
            dimension_semantics=("arbitrary", "arbitrary"), vmem_limit_bytes=VMEM_LIMIT_BYTES),
        name="mla",
    )(c, qt, k, km, qt8, k8, km8, vt, vtm, gmt)


NA_DR = 2 * NA_WIN_R - 1
NA_DC = 2 * NA_WIN_C - 1


def _na_pair_plan():
    r = np.arange(ROWS)
    r0 = np.clip(r - NA_WIN_R // 2, 0, ROWS - NA_WIN_R)
    r0e = np.minimum(r0 - (r0 % 2), ROWS - NA_SLAB_ROWS)[0::2]
    r0e_row = np.repeat(r0e, 2)
    kr = r0e_row[:, None] + np.arange(NA_SLAB_ROWS)[None, :]
    in_win = (kr >= r0[:, None]) & (kr < r0[:, None] + NA_WIN_R)
    assert (in_win.sum(axis=1) == NA_WIN_R).all()
    dr = kr - r[:, None] + (NA_WIN_R - 1)
    blk = np.where(in_win, dr, NA_DR)
    assert blk.min() >= 0 and blk.max() <= NA_DR
    return r0e.astype(np.int32), blk.reshape(-1).astype(np.int32)


def _na_score_bounds(q_w, k_w, rel_bias, meta_bias):
    qk = 1.02 * NA_HEAD_DIM ** -0.5 * LOG2E * NA_HEAD_DIM * jnp.max(jnp.abs(q_w.astype(f32))) * jnp.max(
        jnp.abs(k_w.astype(f32)))
    b_hi = LOG2E * jnp.maximum(jnp.max(rel_bias), jnp.max(meta_bias)).astype(f32)
    b_lo = LOG2E * jnp.minimum(jnp.min(rel_bias), jnp.min(meta_bias)).astype(f32)
    bounded = 2.0 * qk + (b_hi - b_lo) <= NA_FIXED_OFFSET_SPAN
    return qk + b_hi, bounded


def _na_tables(rel_bias, meta_bias, shift):
    c = np.arange(GRID_W)
    c0 = np.clip(c - NA_WIN_C // 2, 0, GRID_W - NA_WIN_C)
    kc = np.arange(GRID_W)
    valid_c = (kc[:, None] >= c0[None, :]) & (kc[:, None] < c0[None, :] + NA_WIN_C)
    dc = kc[:, None] - c[None, :] + (NA_WIN_C - 1)
    sel = (np.arange(NA_DC)[:, None, None] == dc[None]) & valid_c[None]
    toe = jnp.einsum("hrd,dkc->hrkc", rel_bias.astype(f32), jnp.asarray(sel, f32),
                     precision=lax.Precision.HIGHEST)
    t = jnp.where(valid_c[None, None], toe * LOG2E - shift, NEG)
    t = jnp.concatenate([t, jnp.full((NA_HEADS, 1, GRID_W, GRID_W), NEG, f32)], axis=1)
    ng = NA_HEADS // NA_GROUP
    t = t.reshape(ng, NA_GROUP, NA_DR + 1, GRID_W, GRID_W)
    t = t.transpose(0, 2, 3, 1, 4).reshape(ng, NA_DR + 1, GRID_W, NA_GROUP * GRID_W)
    mb = meta_bias.astype(f32) * LOG2E - shift
    mb = jnp.concatenate([mb, jnp.full((NA_HEADS, META_PAD - N_META), NEG, f32)], axis=1)
    tm = jnp.broadcast_to(mb.reshape(ng, NA_GROUP, META_PAD, 1), (ng, NA_GROUP, META_PAD, GRID_W))
    tm = tm.transpose(0, 2, 1, 3).reshape(ng, META_PAD, NA_GROUP * GRID_W)
    return t, tm


def _na_kernel(r0e_ref, blk_ref, flag_ref, q_ref, k_ref, vt_ref, km_ref, vtm_ref, t_ref, tm_ref, g_ref,
               wmask_ref, o_ref, s_ref):
    b = pl.program_id(1)
    gw = NA_GROUP * GRID_W
    lane_lo = lax.broadcasted_iota(jnp.int32, (gw, LANES), 1) < GRID_W
    head_even = lax.broadcasted_iota(jnp.int32, (gw, LANES), 0) < NA_HEAD_DIM
    pairs_per_step = NA_ROWS_PER_STEP // 2

    def scores(pi, slot):
        pg = b * pairs_per_step + pi
        rows = pl.ds(pl.multiple_of(pi * 2 * GRID_W, 2 * GRID_W), 2 * GRID_W)
        q2 = q_ref[rows, :]
        qa, qb = q2[:GRID_W], q2[GRID_W:]
        wt = jnp.concatenate([qa] * NA_GROUP + [qb] * NA_GROUP, axis=0) * wmask_ref[...]
        koff = pl.multiple_of(r0e_ref[pg] * GRID_W, LANES)
        keys = jnp.concatenate([k_ref[pl.ds(koff, NA_SLAB), :], km_ref[...]], axis=0)

        def table(rr):
            base = (pg * 2 + rr) * NA_SLAB_ROWS
            return jnp.concatenate([t_ref[0, blk_ref[base + j]] for j in range(NA_SLAB_ROWS)] + [tm_ref[0]], axis=0)

        bias = jnp.concatenate([table(0), table(1)], axis=1)
        s_ref[slot] = lax.dot_general(keys, wt, _NT, preferred_element_type=f32) + bias

    def finish(pi, slot, use_max):
        pg = b * pairs_per_step + pi
        koff = pl.multiple_of(r0e_ref[pg] * GRID_W, LANES)
        s = s_ref[slot]
        if use_max:
            s = s - jnp.max(s, axis=0, keepdims=True)
        p = jnp.exp2(s)
        l = jnp.sum(p, axis=0, keepdims=True)
        vals = jnp.concatenate([vt_ref[:, pl.ds(koff, NA_SLAB)], vtm_ref[...]], axis=1)
        out = jnp.dot(vals, p.astype(bf16), preferred_element_type=f32) * (1.0 / l)
        oa, ob = out[:, :LANES], out[:, LANES:]
        ya = jnp.where(head_even, oa, pltpu.roll(oa, GRID_W, axis=1))
        yb = jnp.where(head_even, pltpu.roll(ob, GRID_W, axis=1), ob)
        y = jnp.where(lane_lo, ya, yb)
        loff = pl.multiple_of(pi * LANES, LANES)
        g = g_ref[:, pl.ds(loff, LANES)].astype(f32)
        o_ref[:, pl.ds(loff, LANES)] = (y * _silu(g)).astype(bf16)

    def sweep(use_max):
        assert NA_PAIRS_PER_TRIP % 2 == 0
        scores(0, 0)

        def trip(t, carry):
            for u in range(NA_PAIRS_PER_TRIP):
                pi = t * NA_PAIRS_PER_TRIP + u
                scores(jnp.minimum(pi + 1, pairs_per_step - 1), (u + 1) % 2)
                finish(pi, u % 2, use_max)
            return carry
        lax.fori_loop(0, pairs_per_step // NA_PAIRS_PER_TRIP, trip, 0)

    bounded = flag_ref[0] != 0

    @pl.when(bounded)
    def _():
        sweep(False)

    @pl.when(jnp.logical_not(bounded))
    def _():
        sweep(True)


def _na(naq, nak, navt, nakm, navtm, t, tmeta, gnt, bounded):
    r0e, blk = _na_pair_plan()
    ng = NA_HEADS // NA_GROUP
    gw = NA_GROUP * GRID_W
    assert gw == LANES
    wmask = jnp.asarray(np.tile(np.kron(np.eye(NA_GROUP), np.ones((GRID_W, NA_HEAD_DIM))), (2, 1)), bf16)
    tq = NA_ROWS_PER_STEP * GRID_W
    grid = (ng, ROWS // NA_ROWS_PER_STEP)
    grid_spec = pltpu.PrefetchScalarGridSpec(
        num_scalar_prefetch=3,
        grid=grid,
        in_specs=[
            pl.BlockSpec((tq, gw), lambda g, b, *_: (b, g)),
            pl.BlockSpec((SEQ, gw), lambda g, b, *_: (0, g)),
            pl.BlockSpec((gw, SEQ), lambda g, b, *_: (g, 0)),
            pl.BlockSpec((META_PAD, gw), lambda g, b, *_: (0, g)),
            pl.BlockSpec((gw, META_PAD), lambda g, b, *_: (g, 0)),
            pl.BlockSpec((1, NA_DR + 1, GRID_W, gw), lambda g, b, *_: (g, 0, 0, 0)),
            pl.BlockSpec((1, META_PAD, gw), lambda g, b, *_: (g, 0, 0)),
            pl.BlockSpec((gw, tq), lambda g, b, *_: (g, b)),
            pl.BlockSpec((2 * gw, gw), lambda g, b, *_: (0, 0)),
        ],
        out_specs=pl.BlockSpec((gw, tq), lambda g, b, *_: (g, b)),
        scratch_shapes=[pltpu.VMEM((2, NA_SLAB + META_PAD, 2 * gw), f32)],
    )
    return pl.pallas_call(
        _na_kernel,
        grid_spec=grid_spec,
        out_shape=jax.ShapeDtypeStruct((NA_WIDTH, SEQ), bf16),
        compiler_params=pltpu.CompilerParams(
            dimension_semantics=("arbitrary", "arbitrary"), vmem_limit_bytes=VMEM_LIMIT_BYTES),
        name="na",
    )(jnp.asarray(r0e), jnp.asarray(blk), bounded.astype(jnp.int32).reshape(1),
      naq, nak, navt, nakm, navtm, t, tmeta, gnt, wmask)


def _out_kernel(x_ref, ym_ref, yn_ref, wm_ref, wn_ref, o_ref):
    acc = lax.dot_general(ym_ref[...], wm_ref[...], _TN, preferred_element_type=f32)
    acc = acc + lax.dot_general(yn_ref[...], wn_ref[...], _TN, preferred_element_type=f32)
    o_ref[...] = x_ref[...] + acc


def _out_proj(x, ymt, ynt, wm, wn, tm):
    n = x.shape[1]
    return pl.pallas_call(
        _out_kernel,
        grid=(n // tm,),
        in_specs=[
            pl.BlockSpec((None, tm, D_MODEL), lambda i: (0, i, 0)),
            pl.BlockSpec((MLA_WIDTH, tm), lambda i: (0, i)),
            pl.BlockSpec((NA_WIDTH, tm), lambda i: (0, i)),
            _const_spec(wm.shape),
            _const_spec(wn.shape),
        ],
        out_specs=pl.BlockSpec((None, tm, D_MODEL), lambda i: (0, i, 0)),
        out_shape=jax.ShapeDtypeStruct((1, n, D_MODEL), f32),
        compiler_params=pltpu.CompilerParams(
            dimension_semantics=("arbitrary",), vmem_limit_bytes=VMEM_LIMIT_BYTES),
        name="out_proj",
    )(x, ymt, ynt, wm, wn)


def _prep_weights(norm_w, w_in, q_lat_norm_w, kv_lat_norm_w, w_uq, w_ukv,
                  mla_qn_w, mla_qpe_w, mla_kn_w, mla_kpe_w, na_q_norm_w, na_k_norm_w):
    sizes = (MLA_Q_RANK, MLA_KV_RANK, MLA_ROPE, MLA_WIDTH, NA_WIDTH, NA_WIDTH, NA_WIDTH, NA_WIDTH)
    o = np.concatenate([[0], np.cumsum(sizes)])
    w_in = w_in.astype(bf16)
    seg = [w_in[:, o[i]:o[i + 1]] for i in range(8)]
    wa = jnp.concatenate([seg[0], seg[1], seg[4], seg[5]], axis=1)
    wbt = jnp.concatenate([seg[3], seg[6], seg[7], seg[2]], axis=1).T
    ukv = w_ukv.reshape(MLA_KV_RANK, MLA_HEADS, 2, MLA_NOPE)
    return {
        "norm_w": norm_w.reshape(1, D_MODEL).astype(f32),
        "wa": wa,
        "wbt": wbt,
        "qlw": q_lat_norm_w.reshape(1, MLA_Q_RANK).astype(f32),
        "kvlw": kv_lat_norm_w.reshape(1, MLA_KV_RANK).astype(f32),
        "wuqt": w_uq.T.astype(bf16),
        "wuk": ukv[:, :, 0, :].reshape(MLA_KV_RANK, MLA_HEADS * MLA_NOPE).astype(bf16),
        "wuvt": ukv[:, :, 1, :].reshape(MLA_KV_RANK, MLA_HEADS * MLA_V).T.astype(bf16),
        "qnw": mla_qn_w.reshape(MLA_NOPE, 1).astype(f32),
        "qpew": mla_qpe_w.reshape(MLA_ROPE, 1).astype(f32),
        "knw": mla_kn_w.reshape(1, MLA_NOPE).astype(f32),
        "kpew": mla_kpe_w.reshape(MLA_ROPE, 1).astype(f32),
        "naqw": jnp.tile(na_q_norm_w.reshape(1, NA_HEAD_DIM), (1, 2)).astype(f32),
        "nakw": jnp.tile(na_k_norm_w.reshape(1, NA_HEAD_DIM), (1, 2)).astype(f32),
    }


def kernel(x, meta_tokens, norm_w, w_in, q_lat_norm_w, kv_lat_norm_w, w_uq, w_ukv,
           mla_qn_w, mla_qpe_w, mla_kn_w, mla_kpe_w, na_q_norm_w, na_k_norm_w,
           na_rel_bias, na_meta_bias, w_out):
    assert x.shape == (1, SEQ, D_MODEL) and norm_w.shape[0] == 1
    w = _prep_weights(norm_w[0], w_in[0], q_lat_norm_w[0], kv_lat_norm_w[0], w_uq[0], w_ukv[0],
                      mla_qn_w[0], mla_qpe_w[0], mla_kn_w[0], mla_kpe_w[0],
                      na_q_norm_w[0], na_k_norm_w[0])

    pos = jnp.arange(N_META + SEQ, dtype=f32)
    inv_freq = ROPE_BASE ** (-(jnp.arange(0, MLA_ROPE, 2, dtype=f32) / MLA_ROPE))
    ang = pos[:, None] * inv_freq[None, :]
    cos_t, sin_t = jnp.cos(ang).T, jnp.sin(ang).T
    pad = ((0, 0), (0, META_PAD - N_META))
    cos_m, sin_m = jnp.pad(cos_t[:, :N_META], pad), jnp.pad(sin_t[:, :N_META], pad)

    xm = jnp.pad(meta_tokens.astype(f32), ((0, META_PAD - N_META), (0, 0)))[None]

    qt, k, qt8, k8, vt, gmt, naq, nak, navt, gnt = _project(x, cos_t[:, N_META:], sin_t[:, N_META:], w, tm=256)
    _, km, _, km8, vtm, _, _, nakm, navtm, _ = _project(xm, cos_m, sin_m, w, tm=META_PAD)

    c = _mla_score_bound(mla_qn_w[0], mla_qpe_w[0], mla_kn_w[0], mla_kpe_w[0])
    ymt = _mla(c, qt, k, km, qt8, k8, km8, vt, vtm, gmt, tq=1024, tk=1024)

    na_off, na_bounded = _na_score_bounds(na_q_norm_w[0], na_k_norm_w[0], na_rel_bias[0], na_meta_bias[0])
    t, tmeta = _na_tables(na_rel_bias[0], na_meta_bias[0], jnp.where(na_bounded, na_off, 0.0))
    ynt = _na(naq, nak, navt, nakm, navtm, t, tmeta, gnt, na_bounded)

    wo = w_out[0].astype(bf16)
    return _out_proj(x, ymt, ynt, wo[:MLA_WIDTH], wo[MLA_WIDTH:], tm=1024)
```

```python
import functools
import math

import numpy as np
import jax
import jax.numpy as jnp
from jax import lax
from jax.experimental import pallas as pl
from jax.experimental.pallas import tpu as pltpu

D_MODEL = 1024
SEQ = 16384
N_META = 16
GRID_W = 64
ROWS = SEQ // GRID_W
EPS = 1e-6

MLA_HEADS = 8
MLA_NOPE = 128
MLA_ROPE = 64
MLA_V = 128
MLA_Q_RANK = 256
MLA_KV_RANK = 256
MLA_WIDTH = MLA_HEADS * MLA_V
ROPE_BASE = 10000.0

NA_HEADS = 16
NA_HEAD_DIM = 64
NA_WIDTH = NA_HEADS * NA_HEAD_DIM
NA_WIN_R = 8
NA_WIN_C = 16

LOG2E = math.log2(math.e)
NEG = -1e30

LANES = 128
MXU_DIM = 256
VMEM_LIMIT_BYTES = 56 * 1024 * 1024

META_PAD = 128
QK_DIM = MXU_DIM
V_ROWS = MLA_V + 16

NA_GROUP = 2
NA_SLAB_ROWS = 10
NA_SLAB = NA_SLAB_ROWS * GRID_W
NA_ROWS_PER_STEP = 128
NA_PAIRS_PER_TRIP = 16

_NT = (((1,), (1,)), ((), ()))
_TN = (((0,), (0,)), ((), ()))

f32 = jnp.float32
bf16 = jnp.bfloat16

MLA_QK_FP8 = jnp.float8_e4m3fn
MLA_QK_SCALE = math.sqrt((MLA_NOPE + MLA_ROPE) ** -0.5 * LOG2E)
MLA_FIXED_OFFSET_MAX = 60.0
MLA_TILES_PER_TRIP = 8
NA_FIXED_OFFSET_SPAN = 120.0


def _rms(x, axis):
    return lax.rsqrt(jnp.mean(x * x, axis=axis, keepdims=True) + EPS)


def _silu(g):
    return g / (1.0 + jnp.exp(-g))


def _proj_kernel(x_ref, cos_ref, sin_ref, nw_ref, wa_ref, wbt_ref, qlw_ref, kvlw_ref,
                 wuqt_ref, wuk_ref, wuvt_ref, qnw_ref, qpew_ref, knw_ref, kpew_ref,
                 naqw_ref, nakw_ref,
                 qt_ref, k_ref, qt8_ref, k8_ref, vt_ref, gmt_ref, naq_ref, nak_ref, navt_ref, gnt_ref):
    tm = x_ref.shape[0]
    x = x_ref[...]
    h = (x * _rms(x, 1) * nw_ref[...]).astype(bf16)
    pa = jnp.dot(h, wa_ref[...], preferred_element_type=f32)
    pbt = lax.dot_general(wbt_ref[...], h, _NT, preferred_element_type=f32)

    gmt_ref[...] = pbt[0:MLA_WIDTH].astype(bf16)
    navt_ref[...] = pbt[MLA_WIDTH:MLA_WIDTH + NA_WIDTH].astype(bf16)
    gnt_ref[...] = pbt[MLA_WIDTH + NA_WIDTH:MLA_WIDTH + 2 * NA_WIDTH].astype(bf16)

    cos = cos_ref[...]
    sin = sin_ref[...]
    half = MLA_ROPE // 2

    kpe = pbt[MLA_WIDTH + 2 * NA_WIDTH:MLA_WIDTH + 2 * NA_WIDTH + MLA_ROPE]
    kpe = kpe * _rms(kpe, 0) * kpew_ref[...]
    k1, k2 = kpe[:half], kpe[half:]
    kro = jnp.concatenate(
        [k1 * cos - k2 * sin, k2 * cos + k1 * sin, jnp.zeros((LANES - MLA_ROPE, tm), f32)], axis=0)
    kpe_nat = kro.T * MLA_QK_SCALE

    qscale = MLA_QK_SCALE
    ql = pa[:, 0:MLA_Q_RANK]
    qln = (ql * _rms(ql, 1) * qlw_ref[...]).astype(bf16)
    qt = lax.dot_general(wuqt_ref[...], qln, _NT, preferred_element_type=f32)
    per_head = MLA_NOPE + MLA_ROPE
    for hh in range(MLA_HEADS):
        base = per_head * hh
        qn = qt[base:base + MLA_NOPE]
        qn = qn * _rms(qn, 0) * qnw_ref[...]
        qp = qt[base + MLA_NOPE:base + per_head]
        qp = qp * _rms(qp, 0) * qpew_ref[...]
        q1, q2 = qp[:half], qp[half:]
        qh = jnp.concatenate([qn, q1 * cos - q2 * sin, q2 * cos + q1 * sin], axis=0) * qscale
        for dst in (qt_ref, qt8_ref):
            dst[hh, 0:per_head, :] = qh.astype(dst.dtype)
            dst[hh, per_head:QK_DIM, :] = jnp.zeros((QK_DIM - per_head, tm), dst.dtype)

    kvl = pa[:, MLA_Q_RANK:MLA_Q_RANK + MLA_KV_RANK]
    kvn = (kvl * _rms(kvl, 1) * kvlw_ref[...]).astype(bf16)
    kn = jnp.dot(kvn, wuk_ref[...], preferred_element_type=f32)
    vt = lax.dot_general(wuvt_ref[...], kvn, _NT, preferred_element_type=f32)
    for hh in range(MLA_HEADS):
        xh = kn[:, MLA_NOPE * hh:MLA_NOPE * (hh + 1)]
        xh = xh * _rms(xh, 1) * knw_ref[...]
        for dst in (k_ref, k8_ref):
            dst[hh, :, 0:MLA_NOPE] = (xh * MLA_QK_SCALE).astype(dst.dtype)
            dst[hh, :, MLA_NOPE:QK_DIM] = kpe_nat.astype(dst.dtype)
        vt_ref[hh, 0:MLA_V, :] = vt[MLA_V * hh:MLA_V * (hh + 1)].astype(bf16)
        vt_ref[hh, MLA_V:V_ROWS, :] = jnp.ones((V_ROWS - MLA_V, tm), bf16)

    lane = lax.broadcasted_iota(jnp.int32, (tm, LANES), 1)
    lo = lane < NA_HEAD_DIM
    na_scale = NA_HEAD_DIM ** -0.5 * LOG2E
    q_off = MLA_Q_RANK + MLA_KV_RANK
    k_off = q_off + NA_WIDTH
    for p in range(NA_WIDTH // LANES):
        for off, w_ref, dst, scale in ((q_off, naqw_ref, naq_ref, na_scale), (k_off, nakw_ref, nak_ref, 1.0)):
            xp = pa[:, off + LANES * p:off + LANES * (p + 1)]
            sq = xp * xp
            s_lo = jnp.sum(jnp.where(lo, sq, 0.0), axis=1, keepdims=True)
            s_hi = jnp.sum(jnp.where(lo, 0.0, sq), axis=1, keepdims=True)
            r = jnp.where(lo, lax.rsqrt(s_lo / NA_HEAD_DIM + EPS), lax.rsqrt(s_hi / NA_HEAD_DIM + EPS))
            y = xp * r * w_ref[...]
            if scale != 1.0:
                y = y * scale
            dst[:, LANES * p:LANES * (p + 1)] = y.astype(bf16)


def _const_spec(shape):
    nd = len(shape)
    return pl.BlockSpec(shape, lambda i, _nd=nd: (0,) * _nd, pipeline_mode=pl.Buffered(1))


def _project(x, cos_t, sin_t, w, tm):
    n = x.shape[1]
    grid = (n // tm,)
    weights = (w["norm_w"], w["wa"], w["wbt"], w["qlw"], w["kvlw"], w["wuqt"], w["wuk"], w["wuvt"],
               w["qnw"], w["qpew"], w["knw"], w["kpew"], w["naqw"], w["nakw"])
    in_specs = [pl.BlockSpec((None, tm, D_MODEL), lambda i: (0, i, 0)),
                pl.BlockSpec((MLA_ROPE // 2, tm), lambda i: (0, i)),
                pl.BlockSpec((MLA_ROPE // 2, tm), lambda i: (0, i))]
    in_specs += [_const_spec(a.shape) for a in weights]
    out_shape = (
        jax.ShapeDtypeStruct((MLA_HEADS, QK_DIM, n), bf16),
        jax.ShapeDtypeStruct((MLA_HEADS, n, QK_DIM), bf16),
        jax.ShapeDtypeStruct((MLA_HEADS, QK_DIM, n), MLA_QK_FP8),
        jax.ShapeDtypeStruct((MLA_HEADS, n, QK_DIM), MLA_QK_FP8),
        jax.ShapeDtypeStruct((MLA_HEADS, V_ROWS, n), bf16),
        jax.ShapeDtypeStruct((MLA_WIDTH, n), bf16),
        jax.ShapeDtypeStruct((n, NA_WIDTH), bf16),
        jax.ShapeDtypeStruct((n, NA_WIDTH), bf16),
        jax.ShapeDtypeStruct((NA_WIDTH, n), bf16),
        jax.ShapeDtypeStruct((NA_WIDTH, n), bf16),
    )
    out_specs = (
        pl.BlockSpec((MLA_HEADS, QK_DIM, tm), lambda i: (0, 0, i)),
        pl.BlockSpec((MLA_HEADS, tm, QK_DIM), lambda i: (0, i, 0)),
        pl.BlockSpec((MLA_HEADS, QK_DIM, tm), lambda i: (0, 0, i)),
        pl.BlockSpec((MLA_HEADS, tm, QK_DIM), lambda i: (0, i, 0)),
        pl.BlockSpec((MLA_HEADS, V_ROWS, tm), lambda i: (0, 0, i)),
        pl.BlockSpec((MLA_WIDTH, tm), lambda i: (0, i)),
        pl.BlockSpec((tm, NA_WIDTH), lambda i: (i, 0)),
        pl.BlockSpec((tm, NA_WIDTH), lambda i: (i, 0)),
        pl.BlockSpec((NA_WIDTH, tm), lambda i: (0, i)),
        pl.BlockSpec((NA_WIDTH, tm), lambda i: (0, i)),
    )
    return pl.pallas_call(
        _proj_kernel,
        grid=grid,
        in_specs=in_specs,
        out_specs=out_specs,
        out_shape=out_shape,
        compiler_params=pltpu.CompilerParams(
            dimension_semantics=("arbitrary",), vmem_limit_bytes=VMEM_LIMIT_BYTES),
        name="proj",
    )(x, cos_t, sin_t, *weights)


def _mla_online(qt, k_ref, vt_ref, km_ref, vtm_ref, acc_ref, l_ref, m_ref, mx_ref, s_ref, tk):
    nk = k_ref.shape[1] // tk
    assert nk % 2 == 0 and nk >= 2

    def scores(i, slot):
        off = pl.multiple_of(i * tk, tk)
        s = jnp.dot(k_ref[0, pl.ds(off, tk), :], qt, preferred_element_type=f32)
        s_ref[slot] = s
        mx_ref[slot] = jnp.max(s, axis=0, keepdims=True)

    def accumulate(i, slot):
        off = pl.multiple_of(i * tk, tk)
        m_prev = m_ref[...]
        m_new = jnp.maximum(m_prev, mx_ref[slot])
        alpha = jnp.exp2(m_prev - m_new)
        p = jnp.exp2(s_ref[slot] - m_new).astype(bf16)
        pv = jnp.dot(vt_ref[0, :, pl.ds(off, tk)], p, preferred_element_type=f32)
        acc_ref[...] = alpha * acc_ref[...] + pv
        m_ref[...] = m_new

    scores(0, 0)

    s = jnp.dot(km_ref[0], qt, preferred_element_type=f32)
    row = lax.broadcasted_iota(jnp.int32, s.shape, 0)
    s = jnp.where(row < N_META, s, NEG)
    m0 = jnp.max(s, axis=0, keepdims=True)
    p = jnp.exp2(s - m0).astype(bf16)
    acc_ref[...] = jnp.dot(vtm_ref[0], p, preferred_element_type=f32)
    m_ref[...] = m0

    def body(j, carry):
        scores(2 * j + 1, 1)
        accumulate(2 * j, 0)
        scores(2 * j + 2, 0)
        accumulate(2 * j + 1, 1)
        return carry

    lax.fori_loop(0, nk // 2 - 1, body, 0)
    scores(nk - 1, 1)
    accumulate(nk - 2, 0)
    accumulate(nk - 1, 1)
    l_ref[...] = acc_ref[MLA_V:MLA_V + 1]


def _mla_fixed_offset(c, qt, k_ref, vt_ref, km_ref, vtm_ref, acc_ref, l_ref, tk):
    nk = k_ref.shape[1] // tk
    assert nk % MLA_TILES_PER_TRIP == 0

    sub = 8

    def col_sums(p):
        return jnp.sum(p.reshape(p.shape[0] // sub, sub, p.shape[1]), axis=0)

    s = jnp.dot(km_ref[0], qt, preferred_element_type=f32)
    row = lax.broadcasted_iota(jnp.int32, s.shape, 0)
    p = jnp.exp2(jnp.where(row < N_META, s - c, NEG))
    acc_ref[0:MLA_V] = jnp.dot(vtm_ref[0, 0:MLA_V], p.astype(bf16), preferred_element_type=f32)
    acc_ref[MLA_V:MLA_V + sub] = col_sums(p)

    def tile(i):
        off = pl.multiple_of(i * tk, tk)
        for qb in range(qt.shape[1] // MXU_DIM):
            cols = slice(qb * MXU_DIM, (qb + 1) * MXU_DIM)
            s = jnp.dot(k_ref[0, pl.ds(off, tk), :], qt[:, cols], preferred_element_type=f32)
            p = jnp.exp2(s - c)
            acc_ref[0:MLA_V, cols] += jnp.dot(vt_ref[0, 0:MLA_V, pl.ds(off, tk)], p.astype(bf16),
                                              preferred_element_type=f32)
            acc_ref[MLA_V:MLA_V + sub, cols] += col_sums(p)

    def body(j, carry):
        for u in range(MLA_TILES_PER_TRIP):
            tile(j * MLA_TILES_PER_TRIP + u)
        return carry

    lax.fori_loop(0, nk // MLA_TILES_PER_TRIP, body, 0)
    l_ref[...] = jnp.sum(acc_ref[MLA_V:MLA_V + sub], axis=0, keepdims=True)


def _mla_kernel(c_ref, qt_ref, k_ref, km_ref, qt8_ref, k8_ref, km8_ref, vt_ref, vtm_ref, g_ref, o_ref,
                acc_ref, l_ref, m_ref, mx_ref, s_ref, *, tk):
    c = c_ref[0]
    bounded = c <= MLA_FIXED_OFFSET_MAX

    @pl.when(bounded)
    def _():
        _mla_fixed_offset(c, qt8_ref[0], k8_ref, vt_ref, km8_ref, vtm_ref, acc_ref, l_ref, tk)

    @pl.when(jnp.logical_not(bounded))
    def _():
        _mla_online(qt_ref[0], k_ref, vt_ref, km_ref, vtm_ref, acc_ref, l_ref, m_ref, mx_ref, s_ref, tk)

    o = acc_ref[0:MLA_V] / l_ref[...]
    g = g_ref[...].astype(f32)
    o_ref[...] = (o * _silu(g)).astype(bf16)


def _mla_score_bound(qn_w, qpe_w, kn_w, kpe_w):
    def sq(nope_w, pe_w):
        return MLA_NOPE * jnp.max(nope_w.astype(f32) ** 2) + MLA_ROPE * jnp.max(pe_w.astype(f32) ** 2)
    margin = (1.0 + 2.0 ** -4) ** 2 * 1.01
    slack = 0.5
    return (margin * MLA_QK_SCALE ** 2 * jnp.sqrt(sq(qn_w, qpe_w) * sq(kn_w, kpe_w)) + slack).reshape(1)


def _mla(c, qt, k, km, qt8, k8, km8, vt, vtm, gmt, tq, tk):
    n = qt.shape[2]
    grid = (MLA_HEADS, n // tq)
    return pl.pallas_call(
        functools.partial(_mla_kernel, tk=tk),
        grid=grid,
        in_specs=[
            pl.BlockSpec(memory_space=pltpu.SMEM),
            pl.BlockSpec((1, QK_DIM, tq), lambda h, j: (h, 0, j)),
            pl.BlockSpec((1, n, QK_DIM), lambda h, j: (h, 0, 0)),
            pl.BlockSpec((1, META_PAD, QK_DIM), lambda h, j: (h, 0, 0)),
            pl.BlockSpec((1, QK_DIM, tq), lambda h, j: (h, 0, j)),
            pl.BlockSpec((1, n, QK_DIM), lambda h, j: (h, 0, 0)),
            pl.BlockSpec((1, META_PAD, QK_DIM), lambda h, j: (h, 0, 0)),
            pl.BlockSpec((1, V_ROWS, n), lambda h, j: (h, 0, 0)),
            pl.BlockSpec((1, V_ROWS, META_PAD), lambda h, j: (h, 0, 0)),
            pl.BlockSpec((MLA_V, tq), lambda h, j: (h, j)),
        ],
        out_specs=pl.BlockSpec((MLA_V, tq), lambda h, j: (h, j)),
        out_shape=jax.ShapeDtypeStruct((MLA_WIDTH, n), bf16),
        scratch_shapes=[pltpu.VMEM((V_ROWS, tq), f32), pltpu.VMEM((1, tq), f32), pltpu.VMEM((1, tq), f32),
                        pltpu.VMEM((2, 1, tq), f32), pltpu.VMEM((2, tk, tq), f32)],
        compiler_params=pltpu.CompilerParams(
            dimension_semantics=("arbitrary", "arbitrary"), vmem_limit_bytes=VMEM_LIMIT_BYTES),
        name="mla",
    )(c, qt, k, km, qt8, k8, km8, vt, vtm, gmt)


NA_DR = 2 * NA_WIN_R - 1
NA_DC = 2 * NA_WIN_C - 1


def _na_pair_plan():
    r = np.arange(ROWS)
    r0 = np.clip(r - NA_WIN_R // 2, 0, ROWS - NA_WIN_R)
    r0e = np.minimum(r0 - (r0 % 2), ROWS - NA_SLAB_ROWS)[0::2]
    r0e_row = np.repeat(r0e, 2)
    kr = r0e_row[:, None] + np.arange(NA_SLAB_ROWS)[None, :]
    in_win = (kr >= r0[:, None]) & (kr < r0[:, None] + NA_WIN_R)
    assert (in_win.sum(axis=1) == NA_WIN_R).all()
    dr = kr - r[:, None] + (NA_WIN_R - 1)
    blk = np.where(in_win, dr, NA_DR)
    assert blk.min() >= 0 and blk.max() <= NA_DR
    return r0e.astype(np.int32), blk.reshape(-1).astype(np.int32)


def _na_score_bounds(q_w, k_w, rel_bias, meta_bias):
    qk = 1.02 * NA_HEAD_DIM ** -0.5 * LOG2E * NA_HEAD_DIM * jnp.max(jnp.abs(q_w.astype(f32))) * jnp.max(
        jnp.abs(k_w.astype(f32)))
    b_hi = LOG2E * jnp.maximum(jnp.max(rel_bias), jnp.max(meta_bias)).astype(f32)
    b_lo = LOG2E * jnp.minimum(jnp.min(rel_bias), jnp.min(meta_bias)).astype(f32)
    bounded = 2.0 * qk + (b_hi - b_lo) <= NA_FIXED_OFFSET_SPAN
    return qk + b_hi, bounded


def _na_tables(rel_bias, meta_bias, shift):
    c = np.arange(GRID_W)
    c0 = np.clip(c - NA_WIN_C // 2, 0, GRID_W - NA_WIN_C)
    kc = np.arange(GRID_W)
    valid_c = (kc[:, None] >= c0[None, :]) & (kc[:, None] < c0[None, :] + NA_WIN_C)
    dc = kc[:, None] - c[None, :] + (NA_WIN_C - 1)
    ng = NA_HEADS // NA_GROUP
    sel = (np.arange(NA_DC)[:, None, None] == dc[None]) & valid_c[None]
    sel2 = np.einsum("ab,dkc->adkbc", np.eye(NA_GROUP), sel.astype(np.float32))
    sel2 = sel2.reshape(NA_GROUP * NA_DC, GRID_W, NA_GROUP * GRID_W)
    rel = rel_bias.astype(f32).reshape(ng, NA_GROUP, NA_DR, NA_DC).transpose(0, 2, 1, 3)
    rel = jnp.pad(rel.reshape(ng, NA_DR, NA_GROUP * NA_DC), ((0, 0), (0, 1), (0, 0)))
    toe = jnp.einsum("grx,xky->grky", rel, jnp.asarray(sel2), precision=lax.Precision.HIGHEST)
    valid = np.tile(valid_c, (1, NA_GROUP))[None] & (np.arange(NA_DR + 1) < NA_DR)[:, None, None]
    t = jnp.where(valid[None], toe * LOG2E - shift, NEG)
    mb = meta_bias.astype(f32) * LOG2E - shift
    mb = jnp.concatenate([mb, jnp.full((NA_HEADS, META_PAD - N_META), NEG, f32)], axis=1)
    tm = jnp.broadcast_to(mb.reshape(ng, NA_GROUP, META_PAD, 1), (ng, NA_GROUP, META_PAD, GRID_W))
    tm = tm.transpose(0, 2, 1, 3).reshape(ng, META_PAD, NA_GROUP * GRID_W)
    return t, tm


def _na_kernel(r0e_ref, blk_ref, flag_ref, q_ref, k_ref, vt_ref, km_ref, vtm_ref, t_ref, tm_ref, g_ref,
               wmask_ref, o_ref, s_ref):
    b = pl.program_id(1)
    gw = NA_GROUP * GRID_W
    lane_lo = lax.broadcasted_iota(jnp.int32, (gw, LANES), 1) < GRID_W
    head_even = lax.broadcasted_iota(jnp.int32, (gw, LANES), 0) < NA_HEAD_DIM
    pairs_per_step = NA_ROWS_PER_STEP // 2

    def scores(pi, slot):
        pg = b * pairs_per_step + pi
        rows = pl.ds(pl.multiple_of(pi * 2 * GRID_W, 2 * GRID_W), 2 * GRID_W)
        q2 = q_ref[rows, :]
        qa, qb = q2[:GRID_W], q2[GRID_W:]
        wt = jnp.concatenate([qa] * NA_GROUP + [qb] * NA_GROUP, axis=0) * wmask_ref[...]
        koff = pl.multiple_of(r0e_ref[pg] * GRID_W, LANES)
        keys = jnp.concatenate([k_ref[pl.ds(koff, NA_SLAB), :], km_ref[...]], axis=0)

        def table(rr):
            base = (pg * 2 + rr) * NA_SLAB_ROWS
            return jnp.concatenate([t_ref[0, blk_ref[base + j]] for j in range(NA_SLAB_ROWS)] + [tm_ref[0]], axis=0)

        bias = jnp.concatenate([table(0), table(1)], axis=1)
        s_ref[slot] = lax.dot_general(keys, wt, _NT, preferred_element_type=f32) + bias

    def finish(pi, slot, use_max):
        pg = b * pairs_per_step + pi
        koff = pl.multiple_of(r0e_ref[pg] * GRID_W, LANES)
        s = s_ref[slot]
        if use_max:
            s = s - jnp.max(s, axis=0, keepdims=True)
        p = jnp.exp2(s)
        l = jnp.sum(p, axis=0, keepdims=True)
        vals = jnp.concatenate([vt_ref[:, pl.ds(koff, NA_SLAB)], vtm_ref[...]], axis=1)
        out = jnp.dot(vals, p.astype(bf16), preferred_element_type=f32) * (1.0 / l)
        oa, ob = out[:, :LANES], out[:, LANES:]
        ya = jnp.where(head_even, oa, pltpu.roll(oa, GRID_W, axis=1))
        yb = jnp.where(head_even, pltpu.roll(ob, GRID_W, axis=1), ob)
        y = jnp.where(lane_lo, ya, yb)
        loff = pl.multiple_of(pi * LANES, LANES)
        g = g_ref[:, pl.ds(loff, LANES)].astype(f32)
        o_ref[:, pl.ds(loff, LANES)] = (y * _silu(g)).astype(bf16)

    def sweep(use_max):
        assert NA_PAIRS_PER_TRIP % 2 == 0
        scores(0, 0)

        def trip(t, carry):
            for u in range(NA_PAIRS_PER_TRIP):
                pi = t * NA_PAIRS_PER_TRIP + u
                scores(jnp.minimum(pi + 1, pairs_per_step - 1), (u + 1) % 2)
                finish(pi, u % 2, use_max)
            return carry
        lax.fori_loop(0, pairs_per_step // NA_PAIRS_PER_TRIP, trip, 0)

    bounded = flag_ref[0] != 0

    @pl.when(bounded)
    def _():
        sweep(False)

    @pl.when(jnp.logical_not(bounded))
    def _():
        sweep(True)


def _na(naq, nak, navt, nakm, navtm, t, tmeta, gnt, bounded):
    r0e, blk = _na_pair_plan()
    ng = NA_HEADS // NA_GROUP
    gw = NA_GROUP * GRID_W
    assert gw == LANES
    wmask = jnp.asarray(np.tile(np.kron(np.eye(NA_GROUP), np.ones((GRID_W, NA_HEAD_DIM))), (2, 1)), bf16)
    tq = NA_ROWS_PER_STEP * GRID_W
    grid = (ng, ROWS // NA_ROWS_PER_STEP)
    grid_spec = pltpu.PrefetchScalarGridSpec(
        num_scalar_prefetch=3,
        grid=grid,
        in_specs=[
            pl.BlockSpec((tq, gw), lambda g, b, *_: (b, g)),
            pl.BlockSpec((SEQ, gw), lambda g, b, *_: (0, g)),
            pl.BlockSpec((gw, SEQ), lambda g, b, *_: (g, 0)),
            pl.BlockSpec((META_PAD, gw), lambda g, b, *_: (0, g)),
            pl.BlockSpec((gw, META_PAD), lambda g, b, *_: (g, 0)),
            pl.BlockSpec((1, NA_DR + 1, GRID_W, gw), lambda g, b, *_: (g, 0, 0, 0)),
            pl.BlockSpec((1, META_PAD, gw), lambda g, b, *_: (g, 0, 0)),
            pl.BlockSpec((gw, tq), lambda g, b, *_: (g, b)),
            pl.BlockSpec((2 * gw, gw), lambda g, b, *_: (0, 0)),
        ],
        out_specs=pl.BlockSpec((gw, tq), lambda g, b, *_: (g, b)),
        scratch_shapes=[pltpu.VMEM((2, NA_SLAB + META_PAD, 2 * gw), f32)],
    )
    return pl.pallas_call(
        _na_kernel,
        grid_spec=grid_spec,
        out_shape=jax.ShapeDtypeStruct((NA_WIDTH, SEQ), bf16),
        compiler_params=pltpu.CompilerParams(
            dimension_semantics=("arbitrary", "arbitrary"), vmem_limit_bytes=VMEM_LIMIT_BYTES),
        name="na",
    )(jnp.asarray(r0e), jnp.asarray(blk), bounded.astype(jnp.int32).reshape(1),
      naq, nak, navt, nakm, navtm, t, tmeta, gnt, wmask)


def _out_kernel(x_ref, ym_ref, yn_ref, wm_ref, wn_ref, o_ref):
    acc = lax.dot_general(ym_ref[...], wm_ref[...], _TN, preferred_element_type=f32)
    acc = acc + lax.dot_general(yn_ref[...], wn_ref[...], _TN, preferred_element_type=f32)
    o_ref[...] = x_ref[...] + acc


def _out_proj(x, ymt, ynt, wm, wn, tm):
    n = x.shape[1]
    return pl.pallas_call(
        _out_kernel,
        grid=(n // tm,),
        in_specs=[
            pl.BlockSpec((None, tm, D_MODEL), lambda i: (0, i, 0)),
            pl.BlockSpec((MLA_WIDTH, tm), lambda i: (0, i)),
            pl.BlockSpec((NA_WIDTH, tm), lambda i: (0, i)),
            _const_spec(wm.shape),
            _const_spec(wn.shape),
        ],
        out_specs=pl.BlockSpec((None, tm, D_MODEL), lambda i: (0, i, 0)),
        out_shape=jax.ShapeDtypeStruct((1, n, D_MODEL), f32),
        compiler_params=pltpu.CompilerParams(
            dimension_semantics=("arbitrary",), vmem_limit_bytes=VMEM_LIMIT_BYTES),
        name="out_proj",
    )(x, ymt, ynt, wm, wn)


def _prep_weights(norm_w, w_in, q_lat_norm_w, kv_lat_norm_w, w_uq, w_ukv,
                  mla_qn_w, mla_qpe_w, mla_kn_w, mla_kpe_w, na_q_norm_w, na_k_norm_w):
    sizes = (MLA_Q_RANK, MLA_KV_RANK, MLA_ROPE, MLA_WIDTH, NA_WIDTH, NA_WIDTH, NA_WIDTH, NA_WIDTH)
    o = np.concatenate([[0], np.cumsum(sizes)])
    w_in = w_in.astype(bf16)
    seg = [w_in[:, o[i]:o[i + 1]] for i in range(8)]
    wa = jnp.concatenate([seg[0], seg[1], seg[4], seg[5]], axis=1)
    wbt = jnp.concatenate([seg[3], seg[6], seg[7], seg[2]], axis=1).T
    ukv = w_ukv.reshape(MLA_KV_RANK, MLA_HEADS, 2, MLA_NOPE)
    return {
        "norm_w": norm_w.reshape(1, D_MODEL).astype(f32),
        "wa": wa,
        "wbt": wbt,
        "qlw": q_lat_norm_w.reshape(1, MLA_Q_RANK).astype(f32),
        "kvlw": kv_lat_norm_w.reshape(1, MLA_KV_RANK).astype(f32),
        "wuqt": w_uq.T.astype(bf16),
        "wuk": ukv[:, :, 0, :].reshape(MLA_KV_RANK, MLA_HEADS * MLA_NOPE).astype(bf16),
        "wuvt": ukv[:, :, 1, :].reshape(MLA_KV_RANK, MLA_HEADS * MLA_V).T.astype(bf16),
        "qnw": mla_qn_w.reshape(MLA_NOPE, 1).astype(f32),
        "qpew": mla_qpe_w.reshape(MLA_ROPE, 1).astype(f32),
        "knw": mla_kn_w.reshape(1, MLA_NOPE).astype(f32),
        "kpew": mla_kpe_w.reshape(MLA_ROPE, 1).astype(f32),
        "naqw": jnp.tile(na_q_norm_w.reshape(1, NA_HEAD_DIM), (1, 2)).astype(f32),
        "nakw": jnp.tile(na_k_norm_w.reshape(1, NA_HEAD_DIM), (1, 2)).astype(f32),
    }


def kernel(x, meta_tokens, norm_w, w_in, q_lat_norm_w, kv_lat_norm_w, w_uq, w_ukv,
           mla_qn_w, mla_qpe_w, mla_kn_w, mla_kpe_w, na_q_norm_w, na_k_norm_w,
           na_rel_bias, na_meta_bias, w_out):
    assert x.shape == (1, SEQ, D_MODEL) and norm_w.shape[0] == 1
    w = _prep_weights(norm_w[0], w_in[0], q_lat_norm_w[0], kv_lat_norm_w[0], w_uq[0], w_ukv[0],
                      mla_qn_w[0], mla_qpe_w[0], mla_kn_w[0], mla_kpe_w[0],
                      na_q_norm_w[0], na_k_norm_w[0])

    pos = jnp.arange(N_META + SEQ, dtype=f32)
    inv_freq = ROPE_BASE ** (-(jnp.arange(0, MLA_ROPE, 2, dtype=f32) / MLA_ROPE))
    ang = pos[:, None] * inv_freq[None, :]
    cos_t, sin_t = jnp.cos(ang).T, jnp.sin(ang).T
    pad = ((0, 0), (0, META_PAD - N_META))
    cos_m, sin_m = jnp.pad(cos_t[:, :N_META], pad), jnp.pad(sin_t[:, :N_META], pad)

    xm = jnp.pad(meta_tokens.astype(f32), ((0, META_PAD - N_META), (0, 0)))[None]

    qt, k, qt8, k8, vt, gmt, naq, nak, navt, gnt = _project(x, cos_t[:, N_META:], sin_t[:, N_META:], w, tm=256)
    _, km, _, km8, vtm, _, _, nakm, navtm, _ = _project(xm, cos_m, sin_m, w, tm=META_PAD)

    c = _mla_score_bound(mla_qn_w[0], mla_qpe_w[0], mla_kn_w[0], mla_kpe_w[0])
    ymt = _mla(c, qt, k, km, qt8, k8, km8, vt, vtm, gmt, tq=1024, tk=1024)

    na_off, na_bounded = _na_score_bounds(na_q_norm_w[0], na_k_norm_w[0], na_rel_bias[0], na_meta_bias[0])
    t, tmeta = _na_tables(na_rel_bias[0], na_meta_bias[0], jnp.where(na_bounded, na_off, 0.0))
    ynt = _na(naq, nak, navt, nakm, navtm, t, tmeta, gnt, na_bounded)

    wo = w_out[0].astype(bf16)
    return _out_proj(x, ymt, ynt, wo[:MLA_WIDTH], wo[MLA_WIDTH:], tm=1024)
```

```python
import functools
import math

import numpy as np
import jax
import jax.numpy as jnp
from jax import lax
from jax.experimental import pallas as pl
from jax.experimental.pallas import tpu as pltpu

D_MODEL = 1024
SEQ = 16384
N_META = 16
GRID_W = 64
ROWS = SEQ // GRID_W
EPS = 1e-6

MLA_HEADS = 8
MLA_NOPE = 128
MLA_ROPE = 64
MLA_V = 128
MLA_Q_RANK = 256
MLA_KV_RANK = 256
MLA_WIDTH = MLA_HEADS * MLA_V
ROPE_BASE = 10000.0

NA_HEADS = 16
NA_HEAD_DIM = 64
NA_WIDTH = NA_HEADS * NA_HEAD_DIM
NA_WIN_R = 8
NA_WIN_C = 16

LOG2E = math.log2(math.e)
NEG = -1e30

LANES = 128
MXU_DIM = 256
VMEM_LIMIT_BYTES = 56 * 1024 * 1024

META_PAD = 128
QK_DIM = MXU_DIM
V_ROWS = MLA_V + 16

NA_GROUP = 2
NA_SLAB_ROWS = 10
NA_SLAB = NA_SLAB_ROWS * GRID_W
NA_ROWS_PER_STEP = 128
NA_PAIRS_PER_TRIP = 32

_NT = (((1,), (1,)), ((), ()))
_TN = (((0,), (0,)), ((), ()))

f32 = jnp.float32
bf16 = jnp.bfloat16

MLA_QK_FP8 = jnp.float8_e4m3fn
MLA_QK_SCALE = math.sqrt((MLA_NOPE + MLA_ROPE) ** -0.5 * LOG2E)
MLA_FIXED_OFFSET_MAX = 60.0
MLA_TILES_PER_TRIP = 8
NA_FIXED_OFFSET_SPAN = 120.0


def _rms(x, axis):
    return lax.rsqrt(jnp.mean(x * x, axis=axis, keepdims=True) + EPS)


def _silu(g):
    return g / (1.0 + jnp.exp(-g))


def _proj_kernel(x_ref, cos_ref, sin_ref, nw_ref, wa_ref, wbt_ref, qlw_ref, kvlw_ref,
                 wuqt_ref, wuk_ref, wuvt_ref, qnw_ref, qpew_ref, knw_ref, kpew_ref,
                 naqw_ref, nakw_ref,
                 qt_ref, k_ref, qt8_ref, k8_ref, vt_ref, gmt_ref, naq_ref, nak_ref, navt_ref, gnt_ref):
    tm = x_ref.shape[0]
    x = x_ref[...]
    h = (x * _rms(x, 1) * nw_ref[...]).astype(bf16)
    pa = jnp.dot(h, wa_ref[...], preferred_element_type=f32)
    pbt = lax.dot_general(wbt_ref[...], h, _NT, preferred_element_type=f32)

    gmt_ref[...] = pbt[0:MLA_WIDTH].astype(bf16)
    navt_ref[...] = pbt[MLA_WIDTH:MLA_WIDTH + NA_WIDTH].astype(bf16)
    gnt_ref[...] = pbt[MLA_WIDTH + NA_WIDTH:MLA_WIDTH + 2 * NA_WIDTH].astype(bf16)

    cos = cos_ref[...]
    sin = sin_ref[...]
    half = MLA_ROPE // 2

    kpe = pbt[MLA_WIDTH + 2 * NA_WIDTH:MLA_WIDTH + 2 * NA_WIDTH + MLA_ROPE]
    kpe = kpe * _rms(kpe, 0) * kpew_ref[...]
    k1, k2 = kpe[:half], kpe[half:]
    kro = jnp.concatenate(
        [k1 * cos - k2 * sin, k2 * cos + k1 * sin, jnp.zeros((LANES - MLA_ROPE, tm), f32)], axis=0)
    kpe_nat = kro.T * MLA_QK_SCALE

    qscale = MLA_QK_SCALE
    ql = pa[:, 0:MLA_Q_RANK]
    qln = (ql * _rms(ql, 1) * qlw_ref[...]).astype(bf16)
    qt = lax.dot_general(wuqt_ref[...], qln, _NT, preferred_element_type=f32)
    per_head = MLA_NOPE + MLA_ROPE
    for hh in range(MLA_HEADS):
        base = per_head * hh
        qn = qt[base:base + MLA_NOPE]
        qn = qn * _rms(qn, 0) * qnw_ref[...]
        qp = qt[base + MLA_NOPE:base + per_head]
        qp = qp * _rms(qp, 0) * qpew_ref[...]
        q1, q2 = qp[:half], qp[half:]
        qh = jnp.concatenate([qn, q1 * cos - q2 * sin, q2 * cos + q1 * sin], axis=0) * qscale
        for dst in (qt_ref, qt8_ref):
            dst[hh, 0:per_head, :] = qh.astype(dst.dtype)
            dst[hh, per_head:QK_DIM, :] = jnp.zeros((QK_DIM - per_head, tm), dst.dtype)

    kvl = pa[:, MLA_Q_RANK:MLA_Q_RANK + MLA_KV_RANK]
    kvn = (kvl * _rms(kvl, 1) * kvlw_ref[...]).astype(bf16)
    kn = jnp.dot(kvn, wuk_ref[...], preferred_element_type=f32)
    vt = lax.dot_general(wuvt_ref[...], kvn, _NT, preferred_element_type=f32)
    for hh in range(MLA_HEADS):
        xh = kn[:, MLA_NOPE * hh:MLA_NOPE * (hh + 1)]
        xh = xh * _rms(xh, 1) * knw_ref[...]
        for dst in (k_ref, k8_ref):
            dst[hh, :, 0:MLA_NOPE] = (xh * MLA_QK_SCALE).astype(dst.dtype)
            dst[hh, :, MLA_NOPE:QK_DIM] = kpe_nat.astype(dst.dtype)
        vt_ref[hh, 0:MLA_V, :] = vt[MLA_V * hh:MLA_V * (hh + 1)].astype(bf16)
        vt_ref[hh, MLA_V:V_ROWS, :] = jnp.ones((V_ROWS - MLA_V, tm), bf16)

    lane = lax.broadcasted_iota(jnp.int32, (tm, LANES), 1)
    lo = lane < NA_HEAD_DIM
    na_scale = NA_HEAD_DIM ** -0.5 * LOG2E
    q_off = MLA_Q_RANK + MLA_KV_RANK
    k_off = q_off + NA_WIDTH
    for p in range(NA_WIDTH // LANES):
        for off, w_ref, dst, scale in ((q_off, naqw_ref, naq_ref, na_scale), (k_off, nakw_ref, nak_ref, 1.0)):
            xp = pa[:, off + LANES * p:off + LANES * (p + 1)]
            sq = xp * xp
            s_lo = jnp.sum(jnp.where(lo, sq, 0.0), axis=1, keepdims=True)
            s_hi = jnp.sum(jnp.where(lo, 0.0, sq), axis=1, keepdims=True)
            r = jnp.where(lo, lax.rsqrt(s_lo / NA_HEAD_DIM + EPS), lax.rsqrt(s_hi / NA_HEAD_DIM + EPS))
            y = xp * r * w_ref[...]
            if scale != 1.0:
                y = y * scale
            dst[:, LANES * p:LANES * (p + 1)] = y.astype(bf16)


def _const_spec(shape):
    nd = len(shape)
    return pl.BlockSpec(shape, lambda i, _nd=nd: (0,) * _nd, pipeline_mode=pl.Buffered(1))


def _project(x, cos_t, sin_t, w, tm):
    n = x.shape[1]
    grid = (n // tm,)
    weights = (w["norm_w"], w["wa"], w["wbt"], w["qlw"], w["kvlw"], w["wuqt"], w["wuk"], w["wuvt"],
               w["qnw"], w["qpew"], w["knw"], w["kpew"], w["naqw"], w["nakw"])
    in_specs = [pl.BlockSpec((None, tm, D_MODEL), lambda i: (0, i, 0)),
                pl.BlockSpec((MLA_ROPE // 2, tm), lambda i: (0, i)),
                pl.BlockSpec((MLA_ROPE // 2, tm), lambda i: (0, i))]
    in_specs += [_const_spec(a.shape) for a in weights]
    out_shape = (
        jax.ShapeDtypeStruct((MLA_HEADS, QK_DIM, n), bf16),
        jax.ShapeDtypeStruct((MLA_HEADS, n, QK_DIM), bf16),
        jax.ShapeDtypeStruct((MLA_HEADS, QK_DIM, n), MLA_QK_FP8),
        jax.ShapeDtypeStruct((MLA_HEADS, n, QK_DIM), MLA_QK_FP8),
        jax.ShapeDtypeStruct((MLA_HEADS, V_ROWS, n), bf16),
        jax.ShapeDtypeStruct((MLA_WIDTH, n), bf16),
        jax.ShapeDtypeStruct((n, NA_WIDTH), bf16),
        jax.ShapeDtypeStruct((n, NA_WIDTH), bf16),
        jax.ShapeDtypeStruct((NA_WIDTH, n), bf16),
        jax.ShapeDtypeStruct((NA_WIDTH, n), bf16),
    )
    out_specs = (
        pl.BlockSpec((MLA_HEADS, QK_DIM, tm), lambda i: (0, 0, i)),
        pl.BlockSpec((MLA_HEADS, tm, QK_DIM), lambda i: (0, i, 0)),
        pl.BlockSpec((MLA_HEADS, QK_DIM, tm), lambda i: (0, 0, i)),
        pl.BlockSpec((MLA_HEADS, tm, QK_DIM), lambda i: (0, i, 0)),
        pl.BlockSpec((MLA_HEADS, V_ROWS, tm), lambda i: (0, 0, i)),
        pl.BlockSpec((MLA_WIDTH, tm), lambda i: (0, i)),
        pl.BlockSpec((tm, NA_WIDTH), lambda i: (i, 0)),
        pl.BlockSpec((tm, NA_WIDTH), lambda i: (i, 0)),
        pl.BlockSpec((NA_WIDTH, tm), lambda i: (0, i)),
        pl.BlockSpec((NA_WIDTH, tm), lambda i: (0, i)),
    )
    return pl.pallas_call(
        _proj_kernel,
        grid=grid,
        in_specs=in_specs,
        out_specs=out_specs,
        out_shape=out_shape,
        compiler_params=pltpu.CompilerParams(
            dimension_semantics=("arbitrary",), vmem_limit_bytes=VMEM_LIMIT_BYTES),
        name="proj",
    )(x, cos_t, sin_t, *weights)


def _mla_online(qt, k_ref, vt_ref, km_ref, vtm_ref, acc_ref, l_ref, m_ref, mx_ref, s_ref, tk):
    nk = k_ref.shape[1] // tk
    assert nk % 2 == 0 and nk >= 2

    def scores(i, slot):
        off = pl.multiple_of(i * tk, tk)
        s = jnp.dot(k_ref[0, pl.ds(off, tk), :], qt, preferred_element_type=f32)
        s_ref[slot] = s
        mx_ref[slot] = jnp.max(s, axis=0, keepdims=True)

    def accumulate(i, slot):
        off = pl.multiple_of(i * tk, tk)
        m_prev = m_ref[...]
        m_new = jnp.maximum(m_prev, mx_ref[slot])
        alpha = jnp.exp2(m_prev - m_new)
        p = jnp.exp2(s_ref[slot] - m_new).astype(bf16)
        pv = jnp.dot(vt_ref[0, :, pl.ds(off, tk)], p, preferred_element_type=f32)
        acc_ref[...] = alpha * acc_ref[...] + pv
        m_ref[...] = m_new

    scores(0, 0)

    s = jnp.dot(km_ref[0], qt, preferred_element_type=f32)
    row = lax.broadcasted_iota(jnp.int32, s.shape, 0)
    s = jnp.where(row < N_META, s, NEG)
    m0 = jnp.max(s, axis=0, keepdims=True)
    p = jnp.exp2(s - m0).astype(bf16)
    acc_ref[...] = jnp.dot(vtm_ref[0], p, preferred_element_type=f32)
    m_ref[...] = m0

    def body(j, carry):
        scores(2 * j + 1, 1)
        accumulate(2 * j, 0)
        scores(2 * j + 2, 0)
        accumulate(2 * j + 1, 1)
        return carry

    lax.fori_loop(0, nk // 2 - 1, body, 0)
    scores(nk - 1, 1)
    accumulate(nk - 2, 0)
    accumulate(nk - 1, 1)
    l_ref[...] = acc_ref[MLA_V:MLA_V + 1]


def _mla_fixed_offset(c, qt, k_ref, vt_ref, km_ref, vtm_ref, acc_ref, l_ref, tk):
    nk = k_ref.shape[1] // tk
    assert nk % MLA_TILES_PER_TRIP == 0

    sub = 8

    def col_sums(p):
        return jnp.sum(p.reshape(p.shape[0] // sub, sub, p.shape[1]), axis=0)

    s = jnp.dot(km_ref[0], qt, preferred_element_type=f32)
    row = lax.broadcasted_iota(jnp.int32, s.shape, 0)
    p = jnp.exp2(jnp.where(row < N_META, s - c, NEG))
    acc_ref[0:MLA_V] = jnp.dot(vtm_ref[0, 0:MLA_V], p.astype(bf16), preferred_element_type=f32)
    acc_ref[MLA_V:MLA_V + sub] = col_sums(p)

    def tile(i):
        off = pl.multiple_of(i * tk, tk)
        for qb in range(qt.shape[1] // MXU_DIM):
            cols = slice(qb * MXU_DIM, (qb + 1) * MXU_DIM)
            s = jnp.dot(k_ref[0, pl.ds(off, tk), :], qt[:, cols], preferred_element_type=f32)
            p = jnp.exp2(s - c)
            acc_ref[0:MLA_V, cols] += jnp.dot(vt_ref[0, 0:MLA_V, pl.ds(off, tk)], p.astype(bf16),
                                              preferred_element_type=f32)
            acc_ref[MLA_V:MLA_V + sub, cols] += col_sums(p)

    def body(j, carry):
        for u in range(MLA_TILES_PER_TRIP):
            tile(j * MLA_TILES_PER_TRIP + u)
        return carry

    lax.fori_loop(0, nk // MLA_TILES_PER_TRIP, body, 0)
    l_ref[...] = jnp.sum(acc_ref[MLA_V:MLA_V + sub], axis=0, keepdims=True)


def _mla_kernel(c_ref, qt_ref, k_ref, km_ref, qt8_ref, k8_ref, km8_ref, vt_ref, vtm_ref, g_ref, o_ref,
                acc_ref, l_ref, m_ref, mx_ref, s_ref, *, tk):
    c = c_ref[0]
    bounded = c <= MLA_FIXED_OFFSET_MAX

    @pl.when(bounded)
    def _():
        _mla_fixed_offset(c, qt8_ref[0], k8_ref, vt_ref, km8_ref, vtm_ref, acc_ref, l_ref, tk)

    @pl.when(jnp.logical_not(bounded))
    def _():
        _mla_online(qt_ref[0], k_ref, vt_ref, km_ref, vtm_ref, acc_ref, l_ref, m_ref, mx_ref, s_ref, tk)

    o = acc_ref[0:MLA_V] / l_ref[...]
    g = g_ref[...].astype(f32)
    o_ref[...] = (o * _silu(g)).astype(bf16)


def _mla_score_bound(qn_w, qpe_w, kn_w, kpe_w):
    def sq(nope_w, pe_w):
        return MLA_NOPE * jnp.max(nope_w.astype(f32) ** 2) + MLA_ROPE * jnp.max(pe_w.astype(f32) ** 2)
    margin = (1.0 + 2.0 ** -4) ** 2 * 1.01
    slack = 0.5
    return (margin * MLA_QK_SCALE ** 2 * jnp.sqrt(sq(qn_w, qpe_w) * sq(kn_w, kpe_w)) + slack).reshape(1)


def _mla(c, qt, k, km, qt8, k8, km8, vt, vtm, gmt, tq, tk):
    n = qt.shape[2]
    grid = (MLA_HEADS, n // tq)
    return pl.pallas_call(
        functools.partial(_mla_kernel, tk=tk),
        grid=grid,
        in_specs=[
            pl.BlockSpec(memory_space=pltpu.SMEM),
            pl.BlockSpec((1, QK_DIM, tq), lambda h, j: (h, 0, j)),
            pl.BlockSpec((1, n, QK_DIM), lambda h, j: (h, 0, 0)),
            pl.BlockSpec((1, META_PAD, QK_DIM), lambda h, j: (h, 0, 0)),
            pl.BlockSpec((1, QK_DIM, tq), lambda h, j: (h, 0, j)),
            pl.BlockSpec((1, n, QK_DIM), lambda h, j: (h, 0, 0)),
            pl.BlockSpec((1, META_PAD, QK_DIM), lambda h, j: (h, 0, 0)),
            pl.BlockSpec((1, V_ROWS, n), lambda h, j: (h, 0, 0)),
            pl.BlockSpec((1, V_ROWS, META_PAD), lambda h, j: (h, 0, 0)),
            pl.BlockSpec((MLA_V, tq), lambda h, j: (h, j)),
        ],
        out_specs=pl.BlockSpec((MLA_V, tq), lambda h, j: (h, j)),
        out_shape=jax.ShapeDtypeStruct((MLA_WIDTH, n), bf16),
        scratch_shapes=[pltpu.VMEM((V_ROWS, tq), f32), pltpu.VMEM((1, tq), f32), pltpu.VMEM((1, tq), f32),
                        pltpu.VMEM((2, 1, tq), f32), pltpu.VMEM((2, tk, tq), f32)],
        compiler_params=pltpu.CompilerParams(
            dimension_semantics=("arbitrary", "arbitrary"), vmem_limit_bytes=VMEM_LIMIT_BYTES),
        name="mla",
    )(c, qt, k, km, qt8, k8, km8, vt, vtm, gmt)


NA_DR = 2 * NA_WIN_R - 1
NA_DC = 2 * NA_WIN_C - 1


def _na_pair_plan():
    r = np.arange(ROWS)
    r0 = np.clip(r - NA_WIN_R // 2, 0, ROWS - NA_WIN_R)
    r0e = np.minimum(r0 - (r0 % 2), ROWS - NA_SLAB_ROWS)[0::2]
    r0e_row = np.repeat(r0e, 2)
    kr = r0e_row[:, None] + np.arange(NA_SLAB_ROWS)[None, :]
    in_win = (kr >= r0[:, None]) & (kr < r0[:, None] + NA_WIN_R)
    assert (in_win.sum(axis=1) == NA_WIN_R).all()
    dr = kr - r[:, None] + (NA_WIN_R - 1)
    blk = np.where(in_win, dr, NA_DR)
    assert blk.min() >= 0 and blk.max() <= NA_DR
    return r0e.astype(np.int32), blk.reshape(-1).astype(np.int32)


def _na_score_bounds(q_w, k_w, rel_bias, meta_bias):
    qk = 1.02 * NA_HEAD_DIM ** -0.5 * LOG2E * NA_HEAD_DIM * jnp.max(jnp.abs(q_w.astype(f32))) * jnp.max(
        jnp.abs(k_w.astype(f32)))
    b_hi = LOG2E * jnp.maximum(jnp.max(rel_bias), jnp.max(meta_bias)).astype(f32)
    b_lo = LOG2E * jnp.minimum(jnp.min(rel_bias), jnp.min(meta_bias)).astype(f32)
    bounded = 2.0 * qk + (b_hi - b_lo) <= NA_FIXED_OFFSET_SPAN
    return qk + b_hi, bounded


def _na_tables(rel_bias, meta_bias, shift):
    c = np.arange(GRID_W)
    c0 = np.clip(c - NA_WIN_C // 2, 0, GRID_W - NA_WIN_C)
    kc = np.arange(GRID_W)
    valid_c = (kc[:, None] >= c0[None, :]) & (kc[:, None] < c0[None, :] + NA_WIN_C)
    dc = kc[:, None] - c[None, :] + (NA_WIN_C - 1)
    ng = NA_HEADS // NA_GROUP
    sel = (np.arange(NA_DC)[:, None, None] == dc[None]) & valid_c[None]
    sel2 = np.einsum("ab,dkc->adkbc", np.eye(NA_GROUP), sel.astype(np.float32))
    sel2 = sel2.reshape(NA_GROUP * NA_DC, GRID_W, NA_GROUP * GRID_W)
    rel = rel_bias.astype(f32).reshape(ng, NA_GROUP, NA_DR, NA_DC).transpose(0, 2, 1, 3)
    rel = jnp.pad(rel.reshape(ng, NA_DR, NA_GROUP * NA_DC), ((0, 0), (0, 1), (0, 0)))
    toe = jnp.einsum("grx,xky->grky", rel, jnp.asarray(sel2), precision=lax.Precision.HIGHEST)
    valid = np.tile(valid_c, (1, NA_GROUP))[None] & (np.arange(NA_DR + 1) < NA_DR)[:, None, None]
    t = jnp.where(valid[None], toe * LOG2E - shift, NEG)
    mb = meta_bias.astype(f32) * LOG2E - shift
    mb = jnp.concatenate([mb, jnp.full((NA_HEADS, META_PAD - N_META), NEG, f32)], axis=1)
    tm = jnp.broadcast_to(mb.reshape(ng, NA_GROUP, META_PAD, 1), (ng, NA_GROUP, META_PAD, GRID_W))
    tm = tm.transpose(0, 2, 1, 3).reshape(ng, META_PAD, NA_GROUP * GRID_W)
    return t, tm


def _na_kernel(r0e_ref, blk_ref, flag_ref, q_ref, k_ref, vt_ref, km_ref, vtm_ref, t_ref, tm_ref, g_ref,
               wmask_ref, o_ref, s_ref):
    b = pl.program_id(1)
    gw = NA_GROUP * GRID_W
    lane_lo = lax.broadcasted_iota(jnp.int32, (gw, LANES), 1) < GRID_W
    head_even = lax.broadcasted_iota(jnp.int32, (gw, LANES), 0) < NA_HEAD_DIM
    pairs_per_step = NA_ROWS_PER_STEP // 2

    def scores(pi, slot):
        pg = b * pairs_per_step + pi
        rows = pl.ds(pl.multiple_of(pi * 2 * GRID_W, 2 * GRID_W), 2 * GRID_W)
        q2 = q_ref[rows, :]
        qa, qb = q2[:GRID_W], q2[GRID_W:]
        wt = jnp.concatenate([qa] * NA_GROUP + [qb] * NA_GROUP, axis=0) * wmask_ref[...]
        koff = pl.multiple_of(r0e_ref[pg] * GRID_W, LANES)
        keys = jnp.concatenate([k_ref[pl.ds(koff, NA_SLAB), :], km_ref[...]], axis=0)

        def table(rr):
            base = (pg * 2 + rr) * NA_SLAB_ROWS
            return jnp.concatenate([t_ref[0, blk_ref[base + j]] for j in range(NA_SLAB_ROWS)] + [tm_ref[0]], axis=0)

        bias = jnp.concatenate([table(0), table(1)], axis=1)
        s_ref[slot] = lax.dot_general(keys, wt, _NT, preferred_element_type=f32) + bias

    def finish(pi, slot, use_max):
        pg = b * pairs_per_step + pi
        koff = pl.multiple_of(r0e_ref[pg] * GRID_W, LANES)
        s = s_ref[slot]
        if use_max:
            s = s - jnp.max(s, axis=0, keepdims=True)
        p = jnp.exp2(s)
        l = jnp.sum(p, axis=0, keepdims=True)
        vals = jnp.concatenate([vt_ref[:, pl.ds(koff, NA_SLAB)], vtm_ref[...]], axis=1)
        out = jnp.dot(vals, p.astype(bf16), preferred_element_type=f32) * (1.0 / l)
        oa, ob = out[:, :LANES], out[:, LANES:]
        ya = jnp.where(head_even, oa, pltpu.roll(oa, GRID_W, axis=1))
        yb = jnp.where(head_even, pltpu.roll(ob, GRID_W, axis=1), ob)
        y = jnp.where(lane_lo, ya, yb)
        loff = pl.multiple_of(pi * LANES, LANES)
        g = g_ref[:, pl.ds(loff, LANES)].astype(f32)
        o_ref[:, pl.ds(loff, LANES)] = (y * _silu(g)).astype(bf16)

    def sweep(use_max):
        assert NA_PAIRS_PER_TRIP % 2 == 0
        scores(0, 0)

        def trip(t, carry):
            for u in range(NA_PAIRS_PER_TRIP):
                pi = t * NA_PAIRS_PER_TRIP + u
                scores(jnp.minimum(pi + 1, pairs_per_step - 1), (u + 1) % 2)
                finish(pi, u % 2, use_max)
            return carry
        lax.fori_loop(0, pairs_per_step // NA_PAIRS_PER_TRIP, trip, 0)

    bounded = flag_ref[0] != 0

    @pl.when(bounded)
    def _():
        sweep(False)

    @pl.when(jnp.logical_not(bounded))
    def _():
        sweep(True)


def _na(naq, nak, navt, nakm, navtm, t, tmeta, gnt, bounded):
    r0e, blk = _na_pair_plan()
    ng = NA_HEADS // NA_GROUP
    gw = NA_GROUP * GRID_W
    assert gw == LANES
    wmask = jnp.asarray(np.tile(np.kron(np.eye(NA_GROUP), np.ones((GRID_W, NA_HEAD_DIM))), (2, 1)), bf16)
    tq = NA_ROWS_PER_STEP * GRID_W
    grid = (ng, ROWS // NA_ROWS_PER_STEP)
    grid_spec = pltpu.PrefetchScalarGridSpec(
        num_scalar_prefetch=3,
        grid=grid,
        in_specs=[
            pl.BlockSpec((tq, gw), lambda g, b, *_: (b, g)),
            pl.BlockSpec((SEQ, gw), lambda g, b, *_: (0, g)),
            pl.BlockSpec((gw, SEQ), lambda g, b, *_: (g, 0)),
            pl.BlockSpec((META_PAD, gw), lambda g, b, *_: (0, g)),
            pl.BlockSpec((gw, META_PAD), lambda g, b, *_: (g, 0)),
            pl.BlockSpec((1, NA_DR + 1, GRID_W, gw), lambda g, b, *_: (g, 0, 0, 0)),
            pl.BlockSpec((1, META_PAD, gw), lambda g, b, *_: (g, 0, 0)),
            pl.BlockSpec((gw, tq), lambda g, b, *_: (g, b)),
            pl.BlockSpec((2 * gw, gw), lambda g, b, *_: (0, 0)),
        ],
        out_specs=pl.BlockSpec((gw, tq), lambda g, b, *_: (g, b)),
        scratch_shapes=[pltpu.VMEM((2, NA_SLAB + META_PAD, 2 * gw), f32)],
    )
    return pl.pallas_call(
        _na_kernel,
        grid_spec=grid_spec,
        out_shape=jax.ShapeDtypeStruct((NA_WIDTH, SEQ), bf16),
        compiler_params=pltpu.CompilerParams(
            dimension_semantics=("arbitrary", "arbitrary"), vmem_limit_bytes=VMEM_LIMIT_BYTES),
        name="na",
    )(jnp.asarray(r0e), jnp.asarray(blk), bounded.astype(jnp.int32).reshape(1),
      naq, nak, navt, nakm, navtm, t, tmeta, gnt, wmask)


def _out_kernel(x_ref, ym_ref, yn_ref, wm_ref, wn_ref, o_ref):
    acc = lax.dot_general(ym_ref[...], wm_ref[...], _TN, preferred_element_type=f32)
    acc = acc + lax.dot_general(yn_ref[...], wn_ref[...], _TN, preferred_element_type=f32)
    o_ref[...] = x_ref[...] + acc


def _out_proj(x, ymt, ynt, wm, wn, tm):
    n = x.shape[1]
    return pl.pallas_call(
        _out_kernel,
        grid=(n // tm,),
        in_specs=[
            pl.BlockSpec((None, tm, D_MODEL), lambda i: (0, i, 0)),
            pl.BlockSpec((MLA_WIDTH, tm), lambda i: (0, i)),
            pl.BlockSpec((NA_WIDTH, tm), lambda i: (0, i)),
            _const_spec(wm.shape),
            _const_spec(wn.shape),
        ],
        out_specs=pl.BlockSpec((None, tm, D_MODEL), lambda i: (0, i, 0)),
        out_shape=jax.ShapeDtypeStruct((1, n, D_MODEL), f32),
        compiler_params=pltpu.CompilerParams(
            dimension_semantics=("arbitrary",), vmem_limit_bytes=VMEM_LIMIT_BYTES),
        name="out_proj",
    )(x, ymt, ynt, wm, wn)


def _prep_weights(norm_w, w_in, q_lat_norm_w, kv_lat_norm_w, w_uq, w_ukv,
                  mla_qn_w, mla_qpe_w, mla_kn_w, mla_kpe_w, na_q_norm_w, na_k_norm_w):
    sizes = (MLA_Q_RANK, MLA_KV_RANK, MLA_ROPE, MLA_WIDTH, NA_WIDTH, NA_WIDTH, NA_WIDTH, NA_WIDTH)
    o = np.concatenate([[0], np.cumsum(sizes)])
    w_in = w_in.astype(bf16)
    seg = [w_in[:, o[i]:o[i + 1]] for i in range(8)]
    wa = jnp.concatenate([seg[0], seg[1], seg[4], seg[5]], axis=1)
    wbt = jnp.concatenate([seg[3], seg[6], seg[7], seg[2]], axis=1).T
    ukv = w_ukv.reshape(MLA_KV_RANK, MLA_HEADS, 2, MLA_NOPE)
    return {
        "norm_w": norm_w.reshape(1, D_MODEL).astype(f32),
        "wa": wa,
        "wbt": wbt,
        "qlw": q_lat_norm_w.reshape(1, MLA_Q_RANK).astype(f32),
        "kvlw": kv_lat_norm_w.reshape(1, MLA_KV_RANK).astype(f32),
        "wuqt": w_uq.T.astype(bf16),
        "wuk": ukv[:, :, 0, :].reshape(MLA_KV_RANK, MLA_HEADS * MLA_NOPE).astype(bf16),
        "wuvt": ukv[:, :, 1, :].reshape(MLA_KV_RANK, MLA_HEADS * MLA_V).T.astype(bf16),
        "qnw": mla_qn_w.reshape(MLA_NOPE, 1).astype(f32),
        "qpew": mla_qpe_w.reshape(MLA_ROPE, 1).astype(f32),
        "knw": mla_kn_w.reshape(1, MLA_NOPE).astype(f32),
        "kpew": mla_kpe_w.reshape(MLA_ROPE, 1).astype(f32),
        "naqw": jnp.tile(na_q_norm_w.reshape(1, NA_HEAD_DIM), (1, 2)).astype(f32),
        "nakw": jnp.tile(na_k_norm_w.reshape(1, NA_HEAD_DIM), (1, 2)).astype(f32),
    }


def kernel(x, meta_tokens, norm_w, w_in, q_lat_norm_w, kv_lat_norm_w, w_uq, w_ukv,
           mla_qn_w, mla_qpe_w, mla_kn_w, mla_kpe_w, na_q_norm_w, na_k_norm_w,
           na_rel_bias, na_meta_bias, w_out):
    assert x.shape == (1, SEQ, D_MODEL) and norm_w.shape[0] == 1
    w = _prep_weights(norm_w[0], w_in[0], q_lat_norm_w[0], kv_lat_norm_w[0], w_uq[0], w_ukv[0],
                      mla_qn_w[0], mla_qpe_w[0], mla_kn_w[0], mla_kpe_w[0],
                      na_q_norm_w[0], na_k_norm_w[0])

    pos = jnp.arange(N_META + SEQ, dtype=f32)
    inv_freq = ROPE_BASE ** (-(jnp.arange(0, MLA_ROPE, 2, dtype=f32) / MLA_ROPE))
    ang = pos[:, None] * inv_freq[None, :]
    cos_t, sin_t = jnp.cos(ang).T, jnp.sin(ang).T
    pad = ((0, 0), (0, META_PAD - N_META))
    cos_m, sin_m = jnp.pad(cos_t[:, :N_META], pad), jnp.pad(sin_t[:, :N_META], pad)

    xm = jnp.pad(meta_tokens.astype(f32), ((0, META_PAD - N_META), (0, 0)))[None]

    qt, k, qt8, k8, vt, gmt, naq, nak, navt, gnt = _project(x, cos_t[:, N_META:], sin_t[:, N_META:], w, tm=512)
    _, km, _, km8, vtm, _, _, nakm, navtm, _ = _project(xm, cos_m, sin_m, w, tm=META_PAD)

    c = _mla_score_bound(mla_qn_w[0], mla_qpe_w[0], mla_kn_w[0], mla_kpe_w[0])
    ymt = _mla(c, qt, k, km, qt8, k8, km8, vt, vtm, gmt, tq=1024, tk=1024)

    na_off, na_bounded = _na_score_bounds(na_q_norm_w[0], na_k_norm_w[0], na_rel_bias[0], na_meta_bias[0])
    t, tmeta = _na_tables(na_rel_bias[0], na_meta_bias[0], jnp.where(na_bounded, na_off, 0.0))
    ynt = _na(naq, nak, navt, nakm, navtm, t, tmeta, gnt, na_bounded)

    wo = w_out[0].astype(bf16)
    return _out_proj(x, ymt, ynt, wo[:MLA_WIDTH], wo[MLA_WIDTH:], tm=1024)
```
